```python
import math
import jax, jax.numpy as jnp
from jax import lax
import numpy as np

D_MODEL = 1024
BATCH = 4
SEQ = 4096
DEPTH = 1

CTX_LEN = 256
GRID_W = 64
N_MOD = 9
D_FF = ((8 * D_MODEL // 3 + 127) // 128) * 128
NORM_EPS = 1e-6
RW_HEAD_DIM = 64
RW_HEADS = D_MODEL // RW_HEAD_DIM
D_RW = RW_HEADS * RW_HEAD_DIM
W_LORA = 64
A_LORA = 64
G_LORA = 128
LN_X_EPS = 64e-5
DECAY_SCALE = math.exp(-0.5)
D_LRU = D_MODEL
LRU_BLOCKS = 4
LRU_BLOCK = D_LRU // LRU_BLOCKS
LRU_CONV = 4
LRU_C = 8.0
RW_SIZES = (D_RW, D_RW, D_RW, W_LORA, W_LORA, A_LORA, A_LORA, G_LORA)
N_RW_SHIFT = sum(RW_SIZES)
IN_SIZES = (N_RW_SHIFT, D_LRU, D_LRU, D_MODEL, D_MODEL)
N_IN = sum(IN_SIZES)

kernel_name = "hybrid_rwkv7_rglru_prefix_dit_block"


def split_cols(p, sizes):
    idx = np.cumsum(np.array(sizes))[:-1].tolist()
    return jnp.split(p, idx, axis=-1)


def rms_norm(x, g):
    xf = x.astype(jnp.float32)
    y = xf * lax.rsqrt(jnp.mean(xf * xf, axis=-1, keepdims=True) + NORM_EPS)
    return (y * g.astype(jnp.float32)).astype(x.dtype)


def modulate(xn, shift, scale):
    return xn * (1 + scale) + shift


def swiglu(h, wg, wu, wd):
    return (jax.nn.silu(h @ wg) * (h @ wu)) @ wd


def ffn_half_step(h, mod, g, wg, wu, wd):
    shift, scale, gate = mod
    hn = modulate(rms_norm(h, g), shift, scale)
    return h + 0.5 * gate * swiglu(hn, wg, wu, wd)


def shift_seq(p):
    pad = jnp.pad(p, ((0, 0), (1, 1), (0, 0)))
    return 0.5 * (pad[:, :-2] + pad[:, 2:])


def shift_grid(p):
    bsz, t, ch = p.shape
    rows = t // GRID_W
    g = p.reshape(bsz, rows, GRID_W, ch)
    pad = jnp.pad(g, ((0, 0), (1, 1), (1, 1), (0, 0)))
    nb = 0.25 * (pad[:, :-2, 1:-1] + pad[:, 2:, 1:-1] + pad[:, 1:-1, :-2] + pad[:, 1:-1, 2:])
    return nb.reshape(bsz, t, ch)


def depthwise_conv_centred(x, w, b):
    k = w.shape[0]
    left = (k - 1) // 2
    right = k - 1 - left
    ch = x.shape[-1]
    y = lax.conv_general_dilated(
        x.astype(jnp.float32), w.astype(jnp.float32)[:, None, :], window_strides=(1,),
        padding=[(left, right)], dimension_numbers=('NWC', 'WIO', 'NWC'), feature_group_count=ch)
    return y + b.astype(jnp.float32)


def rwkv_step(s, inp):
    r, w, k, v, kk, kka = inp
    sa = jnp.einsum('bhvk,bhk->bhv', s, -kk)
    s = s * w[:, :, None, :] + sa[..., None] * kka[:, :, None, :] + v[..., None] * k[:, :, None, :]
    y = jnp.einsum('bhvk,bhk->bhv', s, r)
    return s, y


def rwkv_scan(r, w, k, v, kk, kka, s0, reverse):
    xs = tuple(jnp.moveaxis(u, 1, 0) for u in (r, w, k, v, kk, kka))
    s_fin, ys = lax.scan(rwkv_step, s0, xs, reverse=reverse)
    return jnp.moveaxis(ys, 0, 1), s_fin


def linear_scan(a, u, h0, reverse):
    def comb(e1, e2):
        a1, b1 = e1
        a2, b2 = e2
        return a1 * a2, a2 * b1 + b2
    if reverse:
        a, u = a[:, ::-1], u[:, ::-1]
    acum, bcum = lax.associative_scan(comb, (a, u), axis=1)
    h = bcum + acum * h0[:, None, :]
    h_fin = h[:, -1]
    if reverse:
        h = h[:, ::-1]
    return h, h_fin


def rwkv_branch(z, s0_f, s0_b, need_out, w0, w_up, a0, a_up, g_up, k_k, k_a, r_k, ln_w, ln_b):
    z = z.astype(jnp.float32)
    r, k, v, wd_f, wd_b, ad_f, ad_b, gd = split_cols(z, RW_SIZES)
    bsz, t = z.shape[0], z.shape[1]
    heads = lambda u: u.reshape(bsz, t, RW_HEADS, RW_HEAD_DIM)
    kk = heads(k * k_k)
    kk = kk / jnp.maximum(jnp.sqrt(jnp.sum(kk * kk, axis=-1, keepdims=True)), 1e-12)
    ys = []
    finals = []
    for d, (wd, ad, s0) in enumerate(((wd_f, ad_f, s0_f), (wd_b, ad_b, s0_b))):
        w = jnp.exp(-DECAY_SCALE * jax.nn.sigmoid(w0[d] + jnp.tanh(wd) @ w_up[d]))
        a = jax.nn.sigmoid(a0[d] + ad @ a_up[d])
        kd = k * (1 + (a - 1) * k_a)
        y, s_fin = rwkv_scan(heads(r), heads(w), heads(kd), heads(v), kk, kk * heads(a), s0, d == 1)
        ys.append(y)
        finals.append(s_fin)
    if not need_out:
        return None, finals
    y = ys[0] + ys[1]
    mu = jnp.mean(y, axis=-1, keepdims=True)
    var = jnp.mean(jnp.square(y - mu), axis=-1, keepdims=True)
    y = ((y - mu) * lax.rsqrt(var + LN_X_EPS)).reshape(bsz, t, D_RW) * ln_w + ln_b
    bonus = jnp.sum(heads(r) * heads(k) * r_k, axis=-1, keepdims=True) * heads(v)
    g = jax.nn.sigmoid(gd) @ g_up
    return (y + bonus.reshape(bsz, t, D_RW)) * g, finals


def lru_branch(xl, gl, h0_f, h0_b, need_out, conv_w, conv_b, lam, wa, ba, wx, bx):
    xc = depthwise_conv_centred(xl, conv_w, conv_b)
    bsz, t = xc.shape[0], xc.shape[1]
    xb = xc.reshape(bsz, t, LRU_BLOCKS, LRU_BLOCK)
    hs = []
    finals = []
    for d, h0 in enumerate((h0_f, h0_b)):
        gate_r = jax.nn.sigmoid(jnp.einsum('btni,nij->btnj', xb, wa[d]).reshape(bsz, t, D_LRU) + ba[d])
        gate_i = jax.nn.sigmoid(jnp.einsum('btni,nij->btnj', xb, wx[d]).reshape(bsz, t, D_LRU) + bx[d])
        log_a = LRU_C * gate_r * jax.nn.log_sigmoid(lam[d])
        a = jnp.exp(log_a)
        u = jnp.sqrt(-jnp.expm1(2 * log_a)) * (gate_i * xc)
        h, h_fin = linear_scan(a, u, h0, d == 1)
        hs.append(h)
        finals.append(h_fin)
    if not need_out:
        return None, finals
    return (hs[0] + hs[1]) * jax.nn.gelu(gl.astype(jnp.float32)), finals


def token_mixer(xn, shift_fn, init, need_out, w_in, rw_mu, rw_w0, rw_w_up, rw_a0, rw_a_up, rw_g_up,
                rw_k_k, rw_k_a, rw_r_k, rw_ln_w, rw_ln_b, w_proj_rw, lru_conv_w, lru_conv_b, lru_lam,
                lru_wa, lru_ba, lru_wx, lru_bx, w_proj_lru, w_out):
    p = xn @ w_in
    p_rw, p_lx, p_lg, p_mrw, p_mlru = split_cols(p, IN_SIZES)
    z = p_rw + (shift_fn(p_rw) - p_rw) * rw_mu
    s_f0, s_b0, h_f0, h_b0 = init
    y_rw, (s_f, s_b) = rwkv_branch(z, s_f0, s_b0, need_out, rw_w0, rw_w_up, rw_a0, rw_a_up, rw_g_up,
                                   rw_k_k, rw_k_a, rw_r_k, rw_ln_w, rw_ln_b)
    y_lru, (h_f, h_b) = lru_branch(p_lx, p_lg, h_f0, h_b0, need_out, lru_conv_w, lru_conv_b, lru_lam,
                                   lru_wa, lru_ba, lru_wx, lru_bx)
    final = (s_f, s_b, h_f, h_b)
    if not need_out:
        return None, final
    merged = (jax.nn.sigmoid(p_mrw) * (y_rw @ w_proj_rw)
              + jax.nn.sigmoid(p_mlru) * (y_lru @ w_proj_lru))
    return merged @ w_out, final


def setup_inputs(seed: int = 0) -> dict:
    key = jax.random.key(seed)
    kit = iter(jax.random.split(key, 48))
    L = DEPTH
    f32 = jnp.float32
    nrm = lambda shape, s: jax.random.normal(next(kit), shape, f32) * s
    uni = lambda shape, lo, hi: jax.random.uniform(next(kit), shape, f32, minval=lo, maxval=hi)
    d = D_MODEL
    u_lam = uni((L, 2, D_LRU), 0.9, 0.999)
    s_lam = u_lam ** (1.0 / LRU_C)
    return {
        "x": nrm((BATCH, SEQ, d), 1.0),
        "c": nrm((BATCH, d), 1.0),
        "ctx": nrm((BATCH, CTX_LEN, d), 1.0),
        "c_ctx": nrm((d,), 1.0),
        "w_mod": nrm((L, d, N_MOD * d), 0.5 * d ** -0.5),
        "b_mod": nrm((L, N_MOD * d), 0.01),
        "g_ffn1": 1.0 + nrm((L, d), 0.02),
        "ffn1_wg": nrm((L, d, D_FF), d ** -0.5),
        "ffn1_wu": nrm((L, d, D_FF), d ** -0.5),
        "ffn1_wd": nrm((L, D_FF, d), D_FF ** -0.5),
        "g_mix": 1.0 + nrm((L, d), 0.02),
        "w_in": nrm((L, d, N_IN), d ** -0.5),
        "rw_mu": uni((L, N_RW_SHIFT), 0.0, 1.0),
        "rw_w0": uni((L, 2, D_RW), -3.0, 1.0),
        "rw_w_up": nrm((L, 2, W_LORA, D_RW), 0.5 * W_LORA ** -0.5),
        "rw_a0": nrm((L, 2, D_RW), 0.5),
        "rw_a_up": nrm((L, 2, A_LORA, D_RW), 0.5 * A_LORA ** -0.5),
        "rw_g_up": nrm((L, G_LORA, D_RW), G_LORA ** -0.5),
        "rw_k_k": 0.85 + nrm((L, D_RW), 0.05),
        "rw_k_a": 1.0 + nrm((L, D_RW), 0.05),
        "rw_r_k": nrm((L, RW_HEADS, RW_HEAD_DIM), 0.1),
        "rw_ln_w": 1.0 + nrm((L, D_RW), 0.02),
        "rw_ln_b": nrm((L, D_RW), 0.01),
        "w_proj_rw": nrm((L, D_RW, d), D_RW ** -0.5),
        "lru_conv_w": nrm((L, LRU_CONV, D_LRU), LRU_CONV ** -0.5),
        "lru_conv_b": nrm((L, D_LRU), 0.01),
        "lru_lam": jnp.log(s_lam) - jnp.log1p(-s_lam),
        "lru_wa": nrm((L, 2, LRU_BLOCKS, LRU_BLOCK, LRU_BLOCK), LRU_BLOCK ** -0.5),
        "lru_ba": nrm((L, 2, D_LRU), 0.01),
        "lru_wx": nrm((L, 2, LRU_BLOCKS, LRU_BLOCK, LRU_BLOCK), LRU_BLOCK ** -0.5),
        "lru_bx": nrm((L, 2, D_LRU), 0.01),
        "w_proj_lru": nrm((L, D_LRU, d), D_LRU ** -0.5),
        "w_out": nrm((L, d, d), d ** -0.5),
        "g_ffn2": 1.0 + nrm((L, d), 0.02),
        "ffn2_wg": nrm((L, d, D_FF), d ** -0.5),
        "ffn2_wu": nrm((L, d, D_FF), d ** -0.5),
        "ffn2_wd": nrm((L, D_FF, d), D_FF ** -0.5),
        "g_final": 1.0 + nrm((d,), 0.02),
    }


def reference(x, c, ctx, c_ctx, w_mod, b_mod, g_ffn1, ffn1_wg, ffn1_wu, ffn1_wd, g_mix, w_in, rw_mu,
              rw_w0, rw_w_up, rw_a0, rw_a_up, rw_g_up, rw_k_k, rw_k_a, rw_r_k, rw_ln_w, rw_ln_b,
              w_proj_rw, lru_conv_w, lru_conv_b, lru_lam, lru_wa, lru_ba, lru_wx, lru_bx, w_proj_lru,
              w_out, g_ffn2, ffn2_wg, ffn2_wu, ffn2_wd, g_final):
    bsz = x.shape[0]
    for l in range(DEPTH):
        last = l == DEPTH - 1
        mod_x = jnp.split((jax.nn.silu(c) @ w_mod[l] + b_mod[l])[:, None, :], N_MOD, axis=-1)
        mod_c = jnp.split((jax.nn.silu(c_ctx) @ w_mod[l] + b_mod[l])[None, None, :], N_MOD, axis=-1)
        x = ffn_half_step(x, mod_x[0:3], g_ffn1[l], ffn1_wg[l], ffn1_wu[l], ffn1_wd[l])
        ctx = ffn_half_step(ctx, mod_c[0:3], g_ffn1[l], ffn1_wg[l], ffn1_wu[l], ffn1_wd[l])
        mixer_params = (w_in[l], rw_mu[l], rw_w0[l], rw_w_up[l], rw_a0[l], rw_a_up[l], rw_g_up[l],
                        rw_k_k[l], rw_k_a[l], rw_r_k[l], rw_ln_w[l], rw_ln_b[l], w_proj_rw[l],
                        lru_conv_w[l], lru_conv_b[l], lru_lam[l], lru_wa[l], lru_ba[l], lru_wx[l],
                        lru_bx[l], w_proj_lru[l], w_out[l])
        zero_state = (jnp.zeros((bsz, RW_HEADS, RW_HEAD_DIM, RW_HEAD_DIM), jnp.float32),
                      jnp.zeros((bsz, RW_HEADS, RW_HEAD_DIM, RW_HEAD_DIM), jnp.float32),
                      jnp.zeros((bsz, D_LRU), jnp.float32),
                      jnp.zeros((bsz, D_LRU), jnp.float32))
        cn = modulate(rms_norm(ctx, g_mix[l]), mod_c[3], mod_c[4])
        ctx_mix, ctx_state = token_mixer(cn, shift_seq, zero_state, not last, *mixer_params)
        xn = modulate(rms_norm(x, g_mix[l]), mod_x[3], mod_x[4])
        x_mix, _ = token_mixer(xn, shift_grid, ctx_state, True, *mixer_params)
        x = x + mod_x[5] * x_mix
        x = ffn_half_step(x, mod_x[6:9], g_ffn2[l], ffn2_wg[l], ffn2_wu[l], ffn2_wd[l])
        if not last:
            ctx = ctx + mod_c[5] * ctx_mix
            ctx = ffn_half_step(ctx, mod_c[6:9], g_ffn2[l], ffn2_wg[l], ffn2_wu[l], ffn2_wd[l])
    return rms_norm(x, g_final)
```

```python
import functools
import math

import jax
import jax.numpy as jnp
from jax import lax
from jax.experimental import pallas as pl
from jax.experimental.pallas import tpu as pltpu

F32 = jnp.float32
BF16 = jnp.bfloat16

D_MODEL = 1024
D_FF = 2816
N_MOD = 9
NORM_EPS = 1e-6
HEAD_DIM = 64
N_HEADS = D_MODEL // HEAD_DIM
LN_X_EPS = 64e-5
DECAY_SCALE = math.exp(-0.5)
LRU_BLOCKS = 4
LRU_BLOCK = D_MODEL // LRU_BLOCKS
LRU_C = 8.0
GRID_W = 64
N_RW = 3 * D_MODEL + 4 * 64 + 128
COL_LORA = 3 * D_MODEL
COL_G = COL_LORA + 256
N_IN = N_RW + 4 * D_MODEL

TOK_TILE = 256
CHUNK = 64
PAIR = 2 * HEAD_DIM
SUBLANES = 8
VMEM_LIMIT = 56 * 1024 * 1024


def _params(*sem):
    return pltpu.CompilerParams(dimension_semantics=sem, vmem_limit_bytes=VMEM_LIMIT)


def _resident(shape):
    nd = len(shape)
    return pl.BlockSpec(shape, lambda *_: (0,) * nd, pipeline_mode=pl.Buffered(1))


def _dot(a, b):
    return jnp.dot(a.astype(BF16), b.astype(BF16), preferred_element_type=F32)


def _dot_nt(a, b):
    return lax.dot_general(a.astype(BF16), b.astype(BF16), (((1,), (1,)), ((), ())),
                           preferred_element_type=F32)


def _split2(x):
    hi = x.astype(BF16)
    lo = (x - hi.astype(F32)).astype(BF16)
    return hi, lo


def _split3(x):
    hi = x.astype(BF16)
    r1 = x - hi.astype(F32)
    mid = r1.astype(BF16)
    lo = (r1 - mid.astype(F32)).astype(BF16)
    return hi, mid, lo


def _rms(x, g):
    return x * lax.rsqrt(jnp.mean(x * x, axis=-1, keepdims=True) + NORM_EPS) * g


def _head_sum_mats():
    c = lax.broadcasted_iota(jnp.int32, (D_MODEL, 128), 0) // HEAD_DIM
    h = lax.broadcasted_iota(jnp.int32, (D_MODEL, 128), 1)
    e = (c == h).astype(BF16)
    ht = lax.broadcasted_iota(jnp.int32, (128, D_MODEL), 0)
    ct = lax.broadcasted_iota(jnp.int32, (128, D_MODEL), 1) // HEAD_DIM
    et = (ht == ct).astype(BF16)
    return e, et


def _head_reduce(x, e):
    hi, lo = _split2(x)
    return (jnp.dot(hi, e, preferred_element_type=F32)
            + jnp.dot(lo, e, preferred_element_type=F32))


def _head_expand(s, et):
    hi, mid, lo = _split3(s)
    return (jnp.dot(hi, et, preferred_element_type=F32)
            + jnp.dot(mid, et, preferred_element_type=F32)
            + jnp.dot(lo, et, preferred_element_type=F32))


def _mod_kernel(c_ref, w_ref, b_ref, o_ref):
    c = c_ref[...]
    s = c * jax.nn.sigmoid(c)
    o_ref[...] = _dot(s, w_ref[...]) + b_ref[...]


def _modulation(cc, w_mod, b_mod):
    n = w_mod.shape[1]
    tn = 1152
    return pl.pallas_call(
        _mod_kernel,
        grid=(n // tn,),
        in_specs=[pl.BlockSpec((SUBLANES, D_MODEL), lambda j: (0, 0)),
                  pl.BlockSpec((D_MODEL, tn), lambda j: (0, j)),
                  pl.BlockSpec((1, tn), lambda j: (0, j))],
        out_specs=pl.BlockSpec((SUBLANES, tn), lambda j: (0, j)),
        out_shape=jax.ShapeDtypeStruct((SUBLANES, n), F32),
        compiler_params=_params("parallel"),
        name="modulation",
    )(cc, w_mod, b_mod)


def _ffn_math(h, m, row0, g, wg_ref, wu_ref, wd_ref):
    shift, scale, gate = m[row0:row0 + 1], m[row0 + 1:row0 + 2], m[row0 + 2:row0 + 3]
    hn = (_rms(h, g) * (1.0 + scale) + shift).astype(BF16)
    a = jnp.dot(hn, wg_ref[...], preferred_element_type=F32)
    u = jnp.dot(hn, wu_ref[...], preferred_element_type=F32)
    act = (a * jax.nn.sigmoid(a) * u).astype(BF16)
    y = jnp.dot(act, wd_ref[...], preferred_element_type=F32)
    return h + 0.5 * gate * y


def _ffn1_kernel(x_ref, ctx_ref, mod_ref, g_ref, wg_ref, wu_ref, wd_ref, o_ref):
    is_ctx = pl.program_id(1) == 0
    h = jnp.where(is_ctx, ctx_ref[0], x_ref[0])
    o_ref[0] = _ffn_math(h, mod_ref[0, 0], 0, g_ref[...], wg_ref, wu_ref, wd_ref)


def _ffn1(x, ctx, mod, g, wg, wu, wd):
    bsz, seq, _ = x.shape
    nt = seq // TOK_TILE + 1
    tile = (1, TOK_TILE, D_MODEL)
    return pl.pallas_call(
        _ffn1_kernel,
        grid=(bsz, nt),
        in_specs=[pl.BlockSpec(tile, lambda b, t: (b, jnp.maximum(t - 1, 0), 0)),
                  pl.BlockSpec(tile, lambda b, t: (b, 0, 0)),
                  pl.BlockSpec((1, 1, N_MOD, D_MODEL), lambda b, t: (b, jnp.minimum(t, 1), 0, 0)),
                  _resident((1, D_MODEL)),
                  _resident((D_MODEL, D_FF)), _resident((D_MODEL, D_FF)), _resident((D_FF, D_MODEL))],
        out_specs=pl.BlockSpec(tile, lambda b, t: (b, t, 0)),
        out_shape=jax.ShapeDtypeStruct((bsz, nt * TOK_TILE, D_MODEL), F32),
        compiler_params=_params("parallel", "parallel"),
        name="ffn1",
    )(x, ctx, mod, g, wg, wu, wd)


def _ffn2_kernel(x_ref, mod_ref, g_ref, wg_ref, wu_ref, wd_ref, gf_ref, o_ref):
    y = _ffn_math(x_ref[0], mod_ref[0, 0], 6, g_ref[...], wg_ref, wu_ref, wd_ref)
    o_ref[0] = _rms(y, gf_ref[...])


def _ffn2(x, mod, g, wg, wu, wd, g_final):
    bsz, seq, _ = x.shape
    tile = (1, TOK_TILE, D_MODEL)
    return pl.pallas_call(
        _ffn2_kernel,
        grid=(bsz, seq // TOK_TILE),
        in_specs=[pl.BlockSpec(tile, lambda b, t: (b, t, 0)),
                  pl.BlockSpec((1, 1, N_MOD, D_MODEL), lambda b, t: (b, 1, 0, 0)),
                  _resident((1, D_MODEL)),
                  _resident((D_MODEL, D_FF)), _resident((D_MODEL, D_FF)), _resident((D_FF, D_MODEL)),
                  _resident((1, D_MODEL))],
        out_specs=pl.BlockSpec(tile, lambda b, t: (b, t, 0)),
        out_shape=jax.ShapeDtypeStruct((bsz, seq, D_MODEL), F32),
        compiler_params=_params("parallel", "parallel"),
        name="ffn2",
    )(x, mod, g, wg, wu, wd, g_final)


def _inproj_kernel(h_ref, mod_ref, g_ref, w_ref, prw_ref, plx_ref, plg_ref, pm_ref):
    m = mod_ref[0, 0]
    xn = (_rms(h_ref[0], g_ref[...]) * (1.0 + m[4:5]) + m[3:4]).astype(BF16)
    c0, c1, c2 = N_RW, N_RW + D_MODEL, N_RW + 2 * D_MODEL
    prw_ref[0] = jnp.dot(xn, w_ref[:, 0:c0], preferred_element_type=F32)
    plx_ref[0] = jnp.dot(xn, w_ref[:, c0:c1], preferred_element_type=F32)
    plg_ref[0] = jnp.dot(xn, w_ref[:, c1:c2], preferred_element_type=F32)
    pm_ref[0] = jnp.dot(xn, w_ref[:, c2:N_IN], preferred_element_type=F32)


def _inproj(h1, mod, g, w_in):
    bsz, tok, _ = h1.shape
    widths = (N_RW, D_MODEL, D_MODEL, 2 * D_MODEL)
    return pl.pallas_call(
        _inproj_kernel,
        grid=(bsz, tok // TOK_TILE),
        in_specs=[pl.BlockSpec((1, TOK_TILE, D_MODEL), lambda b, t: (b, t, 0)),
                  pl.BlockSpec((1, 1, N_MOD, D_MODEL), lambda b, t: (b, jnp.minimum(t, 1), 0, 0)),
                  _resident((1, D_MODEL)),
                  _resident((D_MODEL, N_IN))],
        out_specs=[pl.BlockSpec((1, TOK_TILE, w), lambda b, t: (b, t, 0)) for w in widths],
        out_shape=[jax.ShapeDtypeStruct((bsz, tok, w), F32) for w in widths],
        compiler_params=_params("parallel", "parallel"),
        name="inproj",
    )(h1, mod, g, w_in)


def _zshift_kernel(n_tiles, p_ref, up_ref, dn_ref, mu_ref, kk_w_ref, z_ref, kk_ref):
    t = pl.program_id(1)
    is_ctx = t == 0
    p = p_ref[0]
    row = lax.broadcasted_iota(jnp.int32, (TOK_TILE, 1), 0)
    period = jnp.where(is_ctx, TOK_TILE, GRID_W)
    pos = jnp.where(row >= period, row % GRID_W, row)
    first = pos == 0
    last = pos == period - 1
    left = jnp.where(first, 0.0, pltpu.roll(p, 1, 0))
    right = jnp.where(last, 0.0, pltpu.roll(p, TOK_TILE - 1, 0))
    up_halo = jnp.where(t > 1, up_ref[0], 0.0)
    dn_halo = jnp.where(t < n_tiles - 1, dn_ref[0], 0.0)
    up = jnp.concatenate([up_halo, p[:TOK_TILE - GRID_W]], axis=0)
    dn = jnp.concatenate([p[GRID_W:], dn_halo], axis=0)
    nb = jnp.where(is_ctx, 0.5 * (left + right), 0.25 * (up + dn + left + right))
    z = p + (nb - p) * mu_ref[...]
    z_ref[0] = z
    e, et = _head_sum_mats()
    kx = z[:, D_MODEL:2 * D_MODEL] * kk_w_ref[...]
    nrm = jnp.maximum(jnp.sqrt(_head_reduce(kx * kx, e)), 1e-12)
    kk_ref[0] = kx / _head_expand(nrm, et)


def _zshift(p_rw, mu, k_k):
    bsz, tok, _ = p_rw.shape
    nt = tok // TOK_TILE
    rows_per_tile = TOK_TILE // GRID_W
    n_rows = tok // GRID_W
    return pl.pallas_call(
        functools.partial(_zshift_kernel, nt),
        grid=(bsz, nt),
        in_specs=[pl.BlockSpec((1, TOK_TILE, N_RW), lambda b, t: (b, t, 0)),
                  pl.BlockSpec((1, GRID_W, N_RW),
                               lambda b, t: (b, jnp.maximum(t * rows_per_tile - 1, 0), 0)),
                  pl.BlockSpec((1, GRID_W, N_RW),
                               lambda b, t: (b, jnp.minimum((t + 1) * rows_per_tile, n_rows - 1), 0)),
                  pl.BlockSpec((1, N_RW), lambda b, t: (0, 0)),
                  pl.BlockSpec((1, D_MODEL), lambda b, t: (0, 0))],
        out_specs=[pl.BlockSpec((1, TOK_TILE, N_RW), lambda b, t: (b, t, 0)),
                   pl.BlockSpec((1, TOK_TILE, D_MODEL), lambda b, t: (b, t, 0))],
        out_shape=[jax.ShapeDtypeStruct((bsz, tok, N_RW), F32),
                   jax.ShapeDtypeStruct((bsz, tok, D_MODEL), F32)],
        compiler_params=_params("parallel", "parallel"),
        name="zshift",
    )(p_rw, p_rw, p_rw, mu, k_k)


def _scan_order(reverse, n_ctx, n_all, c):
    if not reverse:
        return c
    return jnp.where(c < n_ctx, n_ctx - 1 - c, n_all + n_ctx - 1 - c)


def _rwkv_kernel(reverse, z_ref, kk_ref, w0_ref, wup_ref, a0_ref, aup_ref, ka_ref, y_ref, st_ref):
    c = pl.program_id(1)

    @pl.when(c == 0)
    def _():
        st_ref[...] = jnp.zeros_like(st_ref)

    z = z_ref[0]
    lora = z[:, COL_LORA:COL_LORA + 256]
    lw = -DECAY_SCALE * jax.nn.sigmoid(w0_ref[0] + _dot(jnp.tanh(lora[:, :128]), wup_ref[0]))
    a = jax.nn.sigmoid(a0_ref[0] + _dot(lora[:, 128:], aup_ref[0]))

    ri = lax.broadcasted_iota(jnp.int32, (CHUNK, CHUNK), 0)
    ci = lax.broadcasted_iota(jnp.int32, (CHUNK, CHUNK), 1)
    tri = ((ci >= ri) if reverse else (ci <= ri)).astype(BF16)
    lw_hi, lw_lo = _split2(lw)
    cum = (jnp.dot(tri, lw_hi, preferred_element_type=F32)
           + jnp.dot(tri, lw_lo, preferred_element_type=F32))
    cum_last = cum[0:1] if reverse else cum[CHUNK - 1:CHUNK]

    r = z[:, 0:D_MODEL]
    k = z[:, D_MODEL:2 * D_MODEL]
    v = z[:, 2 * D_MODEL:3 * D_MODEL]
    kk = kk_ref[0]
    e_neg = jnp.exp(-cum)
    g_all = jnp.exp(cum_last)
    at_all = kk * jnp.exp(cum - lw)
    bt_all = (kk * a) * e_neg
    kt_all = (k * (1.0 + (a - 1.0) * ka_ref[...])) * e_neg
    rt_all = r * jnp.exp(cum)

    rr = lax.broadcasted_iota(jnp.int32, (PAIR, PAIR), 0)
    cc = lax.broadcasted_iota(jnp.int32, (PAIR, PAIR), 1)
    same = (rr // HEAD_DIM) == (cc // HEAD_DIM)
    tt, ss = rr % HEAD_DIM, cc % HEAD_DIM
    before = same & ((ss > tt) if reverse else (ss < tt))
    before_eq = same & ((ss >= tt) if reverse else (ss <= tt))
    eye = rr == cc
    lane = lax.broadcasted_iota(jnp.int32, (CHUNK, PAIR), 1)
    head0 = lane < HEAD_DIM

    def stack_masked(x):
        return jnp.concatenate([jnp.where(head0, x, 0.0), jnp.where(head0, 0.0, x)], axis=0)

    def stack(x):
        return jnp.concatenate([x, x], axis=0)

    for p in range(N_HEADS // 2):
        sl = slice(p * PAIR, (p + 1) * PAIR)
        at, bt, kt, rt, vv, g = at_all[:, sl], bt_all[:, sl], kt_all[:, sl], rt_all[:, sl], v[:, sl], g_all[:, sl]
        o = _dot_nt(jnp.concatenate([stack_masked(at), stack_masked(rt)], axis=0),
                    jnp.concatenate([stack(bt), stack(kt)], axis=0))
        a_ab = jnp.where(before, o[:PAIR, :PAIR], 0.0)
        a_ak = jnp.where(before, o[:PAIR, PAIR:], 0.0)
        rb = jnp.where(before_eq, o[PAIR:, :PAIR], 0.0)
        rk = jnp.where(before_eq, o[PAIR:, PAIR:], 0.0)
        inv = jnp.where(eye, 1.0, 0.0) - a_ab
        pw = _dot(a_ab, a_ab)
        for _ in range(4):
            zz = _dot(pw, jnp.concatenate([pw, inv], axis=1))
            pw, inv = zz[:, :PAIR], inv + zz[:, PAIR:]
        inv = inv + _dot(pw, inv)
        v_st = stack(vv)
        wu = -_dot(inv, jnp.concatenate([stack(at), _dot(a_ak, v_st)], axis=1))
        bh = stack_masked(bt * g)
        kh = stack_masked(kt * g)
        lhs = jnp.concatenate([jnp.concatenate([bh.T, kh.T], axis=1),
                               jnp.concatenate([rb, rk], axis=1)], axis=0)
        rhs = jnp.concatenate([wu, jnp.concatenate([jnp.zeros((PAIR, PAIR), F32), v_st], axis=1)], axis=0)
        o = _dot(lhs, rhs)
        m_c = jnp.where(same, o[:PAIR, :PAIR], 0.0) + jnp.where(eye, g, 0.0)
        n_c = jnp.where(same, o[:PAIR, PAIR:], 0.0)
        r_eff = jnp.where(same, o[PAIR:, :PAIR] + stack(rt), 0.0)
        o2 = _dot(jnp.concatenate([m_c, r_eff], axis=0), st_ref[p])
        st_ref[p] = o2[:PAIR] + n_c
        y_st = o2[PAIR:] + o[PAIR:, PAIR:]
        y_ref[0, :, sl] = jnp.where(head0, y_st[:CHUNK], y_st[CHUNK:])


def _rwkv_scan(reverse, z, kk, w0, w_up_ext, a0, a_up_ext, k_a, n_ctx_chunks):
    bsz, tok, _ = z.shape
    n_all = tok // CHUNK
    order = functools.partial(_scan_order, reverse, n_ctx_chunks, n_all)
    return pl.pallas_call(
        functools.partial(_rwkv_kernel, reverse),
        grid=(bsz, n_all),
        in_specs=[pl.BlockSpec((1, CHUNK, N_RW), lambda b, c: (b, order(c), 0)),
                  pl.BlockSpec((1, CHUNK, D_MODEL), lambda b, c: (b, order(c), 0)),
                  pl.BlockSpec((1, 1, D_MODEL), lambda b, c: (0, 0, 0)),
                  pl.BlockSpec((1, 128, D_MODEL), lambda b, c: (0, 0, 0)),
                  pl.BlockSpec((1, 1, D_MODEL), lambda b, c: (0, 0, 0)),
                  pl.BlockSpec((1, 128, D_MODEL), lambda b, c: (0, 0, 0)),
                  pl.BlockSpec((1, D_MODEL), lambda b, c: (0, 0))],
        out_specs=pl.BlockSpec((1, CHUNK, D_MODEL), lambda b, c: (b, order(c), 0)),
        out_shape=jax.ShapeDtypeStruct((bsz, tok, D_MODEL), F32),
        scratch_shapes=[pltpu.VMEM((N_HEADS // 2, PAIR, PAIR), F32)],
        compiler_params=_params("parallel", "arbitrary"),
        name="rwkv_bwd" if reverse else "rwkv_fwd",
    )(z, kk, w0, w_up_ext, a0, a_up_ext, k_a)


def _lru_kernel(reverse, n_tiles, x_ref, prev_ref, next_ref, cw_ref, cb_ref, lam_ref,
                wa_ref, ba_ref, wx_ref, bx_ref, h_ref, carry_ref, a_scr, u_scr):
    t = pl.program_id(1)

    @pl.when(t == 0)
    def _():
        carry_ref[...] = jnp.zeros_like(carry_ref)

    ti = _scan_order(reverse, 1, n_tiles, t)
    prev = jnp.where(ti >= 2, prev_ref[0], 0.0)
    nxt = jnp.where((ti >= 1) & (ti <= n_tiles - 2), next_ref[0], 0.0)
    x = x_ref[0]
    ext = jnp.concatenate([prev, x, nxt], axis=0)
    n_ext = TOK_TILE + 2 * SUBLANES
    body = slice(SUBLANES, SUBLANES + TOK_TILE)
    cw = cw_ref[...]
    xc = (cw[0:1] * pltpu.roll(ext, 1, 0)[body] + cw[1:2] * x
          + cw[2:3] * pltpu.roll(ext, n_ext - 1, 0)[body]
          + cw[3:4] * pltpu.roll(ext, n_ext - 2, 0)[body] + cb_ref[...])
    xb = xc.astype(BF16)
    gr = jnp.concatenate([jnp.dot(xb[:, n * LRU_BLOCK:(n + 1) * LRU_BLOCK], wa_ref[0, n],
                                  preferred_element_type=F32) for n in range(LRU_BLOCKS)], axis=1)
    gi = jnp.concatenate([jnp.dot(xb[:, n * LRU_BLOCK:(n + 1) * LRU_BLOCK], wx_ref[0, n],
                                  preferred_element_type=F32) for n in range(LRU_BLOCKS)], axis=1)
    gate_r = jax.nn.sigmoid(gr + ba_ref[0])
    gate_i = jax.nn.sigmoid(gi + bx_ref[0])
    lam = lam_ref[0]
    log_sig = jnp.minimum(lam, 0.0) - jnp.log1p(jnp.exp(-jnp.abs(lam)))
    log_a = LRU_C * gate_r * log_sig
    a = jnp.exp(log_a)
    a_scr[...] = a
    u_scr[...] = jnp.sqrt(-jnp.tanh(log_a) * (a * a + 1.0)) * (gate_i * xc)

    row = lax.broadcasted_iota(jnp.int32, (SUBLANES, D_MODEL), 0)
    n_groups = TOK_TILE // SUBLANES

    def group(i, h_in):
        gidx = (n_groups - 1 - i) if reverse else i
        rows = pl.ds(pl.multiple_of(gidx * SUBLANES, SUBLANES), SUBLANES)
        a = a_scr[rows, :]
        u = u_scr[rows, :]
        for s in (1, 2, 4):
            if reverse:
                ok = row < SUBLANES - s
                sh = SUBLANES - s
            else:
                ok = row >= s
                sh = s
            u = jnp.where(ok, a * pltpu.roll(u, sh, 0) + u, u)
            a = jnp.where(ok, a * pltpu.roll(a, sh, 0), a)
        h = u + a * h_in
        h_ref[0, rows, :] = h
        last = h[0:1] if reverse else h[SUBLANES - 1:SUBLANES]
        return jnp.broadcast_to(last, (SUBLANES, D_MODEL))

    carry_ref[...] = lax.fori_loop(0, n_groups, group, carry_ref[...])


def _lru_scan(reverse, d, p_lx, conv_w, conv_b, lam, wa, ba, wx, bx):
    bsz, tok, _ = p_lx.shape
    nt = tok // TOK_TILE
    gpt = TOK_TILE // SUBLANES
    n_groups = tok // SUBLANES
    order = functools.partial(_scan_order, reverse, 1, nt)
    dsel = lambda *_: (d, 0, 0)
    return pl.pallas_call(
        functools.partial(_lru_kernel, reverse, nt),
        grid=(bsz, nt),
        in_specs=[pl.BlockSpec((1, TOK_TILE, D_MODEL), lambda b, t: (b, order(t), 0)),
                  pl.BlockSpec((1, SUBLANES, D_MODEL),
                               lambda b, t: (b, jnp.maximum(order(t) * gpt - 1, 0), 0)),
                  pl.BlockSpec((1, SUBLANES, D_MODEL),
                               lambda b, t: (b, jnp.minimum((order(t) + 1) * gpt, n_groups - 1), 0)),
                  pl.BlockSpec((4, D_MODEL), lambda b, t: (0, 0)),
                  pl.BlockSpec((1, D_MODEL), lambda b, t: (0, 0)),
                  pl.BlockSpec((1, 1, D_MODEL), dsel),
                  pl.BlockSpec((1, LRU_BLOCKS, LRU_BLOCK, LRU_BLOCK), lambda b, t: (d, 0, 0, 0)),
                  pl.BlockSpec((1, 1, D_MODEL), dsel),
                  pl.BlockSpec((1, LRU_BLOCKS, LRU_BLOCK, LRU_BLOCK), lambda b, t: (d, 0, 0, 0)),
                  pl.BlockSpec((1, 1, D_MODEL), dsel)],
        out_specs=pl.BlockSpec((1, TOK_TILE, D_MODEL), lambda b, t: (b, order(t), 0)),
        out_shape=jax.ShapeDtypeStruct((bsz, tok, D_MODEL), F32),
        scratch_shapes=[pltpu.VMEM((SUBLANES, D_MODEL), F32),
                        pltpu.VMEM((TOK_TILE, D_MODEL), F32),
                        pltpu.VMEM((TOK_TILE, D_MODEL), F32)],
        compiler_params=_params("parallel", "arbitrary"),
        name="lru_bwd" if reverse else "lru_fwd",
    )(p_lx, p_lx, p_lx, conv_w, conv_b, lam, wa, ba, wx, bx)


def _mixout_kernel(yf_ref, yb_ref, z_ref, hf_ref, hb_ref, plg_ref, pm_ref, h1_ref, mod_ref,
                   rk_ref, lnw_ref, lnb_ref, gup_ref, wprw_ref, wplru_ref, wout_ref, o_ref):
    e, et = _head_sum_mats()
    inv_n = 1.0 / HEAD_DIM
    y = yf_ref[0] + yb_ref[0]
    mu = _head_expand(_head_reduce(y, e) * inv_n, et)
    yc = y - mu
    var = _head_expand(_head_reduce(yc * yc, e) * inv_n, et)
    yn = yc * lax.rsqrt(var + LN_X_EPS) * lnw_ref[...] + lnb_ref[...]
    z = z_ref[0]
    r = z[:, 0:D_MODEL]
    k = z[:, D_MODEL:2 * D_MODEL]
    v = z[:, 2 * D_MODEL:3 * D_MODEL]
    bonus = _head_expand(_head_reduce(r * k * rk_ref[...], e), et) * v
    gate = _dot(jax.nn.sigmoid(z[:, COL_G:COL_G + 128]), gup_ref[...])
    y_rw = (yn + bonus) * gate
    lg = plg_ref[0]
    gelu = lg * (0.5 * (1.0 + jnp.tanh(math.sqrt(2.0 / math.pi) * (lg + 0.044715 * (lg * lg * lg)))))
    y_lru = (hf_ref[0] + hb_ref[0]) * gelu
    pm = pm_ref[0]
    merged = (jax.nn.sigmoid(pm[:, :D_MODEL]) * _dot(y_rw, wprw_ref[...])
              + jax.nn.sigmoid(pm[:, D_MODEL:]) * _dot(y_lru, wplru_ref[...]))
    m = mod_ref[0, 0]
    o_ref[0] = h1_ref[0] + m[5:6] * _dot(merged, wout_ref[...])


def _mixout(y_f, y_b, z, h_f, h_b, p_lg, p_m, h1, mod, r_k, ln_w, ln_b, g_up, w_proj_rw, w_proj_lru, w_out):
    bsz, tok, _ = h1.shape
    nt = tok // TOK_TILE - 1
    lat = lambda w: pl.BlockSpec((1, TOK_TILE, w), lambda b, t: (b, t + 1, 0))
    row = lambda: pl.BlockSpec((1, D_MODEL), lambda b, t: (0, 0))
    return pl.pallas_call(
        _mixout_kernel,
        grid=(bsz, nt),
        in_specs=[lat(D_MODEL), lat(D_MODEL), lat(N_RW), lat(D_MODEL), lat(D_MODEL), lat(D_MODEL),
                  lat(2 * D_MODEL), lat(D_MODEL),
                  pl.BlockSpec((1, 1, N_MOD, D_MODEL), lambda b, t: (b, 1, 0, 0)),
                  row(), row(), row(),
                  _resident((128, D_MODEL)), _resident((D_MODEL, D_MODEL)),
                  _resident((D_MODEL, D_MODEL)), _resident((D_MODEL, D_MODEL))],
        out_specs=pl.BlockSpec((1, TOK_TILE, D_MODEL), lambda b, t: (b, t, 0)),
        out_shape=jax.ShapeDtypeStruct((bsz, nt * TOK_TILE, D_MODEL), F32),
        compiler_params=_params("parallel", "parallel"),
        name="mixout",
    )(y_f, y_b, z, h_f, h_b, p_lg, p_m, h1, mod, r_k, ln_w, ln_b, g_up, w_proj_rw, w_proj_lru, w_out)


def _lora_ext(w_up, d):
    zero = jnp.zeros_like(w_up[0])
    parts = [w_up[0], zero] if d == 0 else [zero, w_up[1]]
    return jnp.concatenate(parts, axis=0)[None].astype(BF16)


def kernel(x, c, ctx, c_ctx, w_mod, b_mod, g_ffn1, ffn1_wg, ffn1_wu, ffn1_wd, g_mix, w_in, rw_mu, rw_w0, rw_w_up, rw_a0, rw_a_up, rw_g_up, rw_k_k, rw_k_a, rw_r_k, rw_ln_w, rw_ln_b, w_proj_rw, lru_conv_w, lru_conv_b, lru_lam, lru_wa, lru_ba, lru_wx, lru_bx, w_proj_lru, w_out, g_ffn2, ffn2_wg, ffn2_wu, ffn2_wd, g_final):
    bsz, seq, d_model = x.shape
    assert d_model == D_MODEL and seq % TOK_TILE == 0 and ctx.shape[1] == TOK_TILE
    assert w_mod.shape[0] == 1 and bsz < SUBLANES
    bf = lambda w: w.astype(BF16)
    row = lambda p: p.reshape(1, -1)

    cc = jnp.zeros((SUBLANES, D_MODEL), F32).at[:bsz].set(c).at[bsz].set(c_ctx)
    mod = _modulation(cc, w_mod[0], row(b_mod[0])).reshape(SUBLANES, N_MOD, D_MODEL)
    mod = jnp.stack([jnp.broadcast_to(mod[bsz], (bsz, N_MOD, D_MODEL)), mod[:bsz]], axis=1)

    h1 = _ffn1(x, ctx, mod, row(g_ffn1[0]), bf(ffn1_wg[0]), bf(ffn1_wu[0]), bf(ffn1_wd[0]))
    p_rw, p_lx, p_lg, p_m = _inproj(h1, mod, row(g_mix[0]), bf(w_in[0]))
    z, kk = _zshift(p_rw, row(rw_mu[0]), row(rw_k_k[0]))

    n_ctx_chunks = TOK_TILE // CHUNK
    ys, hs = [], []
    for d, reverse in ((0, False), (1, True)):
        ys.append(_rwkv_scan(reverse, z, kk, rw_w0[0, d].reshape(1, 1, -1), _lora_ext(rw_w_up[0], d),
                             rw_a0[0, d].reshape(1, 1, -1), _lora_ext(rw_a_up[0], d), row(rw_k_a[0]),
                             n_ctx_chunks))
        hs.append(_lru_scan(reverse, d, p_lx, lru_conv_w[0], row(lru_conv_b[0]),
                            lru_lam[0][:, None, :], bf(lru_wa[0]), lru_ba[0][:, None, :],
                            bf(lru_wx[0]), lru_bx[0][:, None, :]))

    x2 = _mixout(ys[0], ys[1], z, hs[0], hs[1], p_lg, p_m, h1, mod, row(rw_r_k[0]), row(rw_ln_w[0]),
                 row(rw_ln_b[0]), bf(rw_g_up[0]), bf(w_proj_rw[0]), bf(w_proj_lru[0]), bf(w_out[0]))
    return _ffn2(x2, mod, row(g_ffn2[0]), bf(ffn2_wg[0]), bf(ffn2_wu[0]), bf(ffn2_wd[0]), row(g_final))
```

```python
import functools
import math

import jax
import jax.numpy as jnp
from jax import lax
from jax.experimental import pallas as pl
from jax.experimental.pallas import tpu as pltpu

F32 = jnp.float32
BF16 = jnp.bfloat16

D_MODEL = 1024
D_FF = 2816
N_MOD = 9
NORM_EPS = 1e-6
HEAD_DIM = 64
N_HEADS = D_MODEL // HEAD_DIM
LN_X_EPS = 64e-5
DECAY_SCALE = math.exp(-0.5)
LRU_BLOCKS = 4
LRU_BLOCK = D_MODEL // LRU_BLOCKS
LRU_C = 8.0
GRID_W = 64
N_RW = 3 * D_MODEL + 4 * 64 + 128
COL_LORA = 3 * D_MODEL
COL_G = COL_LORA + 256
N_IN = N_RW + 4 * D_MODEL

TOK_TILE = 256
CHUNK = 64
PAIR = 2 * HEAD_DIM
SUBLANES = 8
VMEM_LIMIT = 56 * 1024 * 1024


def _params(*sem):
    return pltpu.CompilerParams(dimension_semantics=sem, vmem_limit_bytes=VMEM_LIMIT)


def _resident(shape):
    nd = len(shape)
    return pl.BlockSpec(shape, lambda *_: (0,) * nd, pipeline_mode=pl.Buffered(1))


def _dot(a, b):
    return jnp.dot(a.astype(BF16), b.astype(BF16), preferred_element_type=F32)


def _dot_nt(a, b):
    return lax.dot_general(a.astype(BF16), b.astype(BF16), (((1,), (1,)), ((), ())),
                           preferred_element_type=F32)


def _split2(x):
    hi = x.astype(BF16)
    lo = (x - hi.astype(F32)).astype(BF16)
    return hi, lo


def _split3(x):
    hi = x.astype(BF16)
    r1 = x - hi.astype(F32)
    mid = r1.astype(BF16)
    lo = (r1 - mid.astype(F32)).astype(BF16)
    return hi, mid, lo


def _rms(x, g):
    return x * lax.rsqrt(jnp.mean(x * x, axis=-1, keepdims=True) + NORM_EPS) * g


def _head_sum_mats():
    c = lax.broadcasted_iota(jnp.int32, (D_MODEL, 128), 0) // HEAD_DIM
    h = lax.broadcasted_iota(jnp.int32, (D_MODEL, 128), 1)
    e = (c == h).astype(BF16)
    ht = lax.broadcasted_iota(jnp.int32, (128, D_MODEL), 0)
    ct = lax.broadcasted_iota(jnp.int32, (128, D_MODEL), 1) // HEAD_DIM
    et = (ht == ct).astype(BF16)
    return e, et


def _head_reduce(x, e):
    hi, lo = _split2(x)
    return (jnp.dot(hi, e, preferred_element_type=F32)
            + jnp.dot(lo, e, preferred_element_type=F32))


def _head_expand(s, et):
    hi, mid, lo = _split3(s)
    return (jnp.dot(hi, et, preferred_element_type=F32)
            + jnp.dot(mid, et, preferred_element_type=F32)
            + jnp.dot(lo, et, preferred_element_type=F32))


def _mod_kernel(c_ref, w_ref, b_ref, o_ref):
    c = c_ref[...]
    s = c * jax.nn.sigmoid(c)
    o_ref[...] = _dot(s, w_ref[...]) + b_ref[...]


def _modulation(cc, w_mod, b_mod):
    n = w_mod.shape[1]
    tn = 1152
    return pl.pallas_call(
        _mod_kernel,
        grid=(n // tn,),
        in_specs=[pl.BlockSpec((SUBLANES, D_MODEL), lambda j: (0, 0)),
                  pl.BlockSpec((D_MODEL, tn), lambda j: (0, j)),
                  pl.BlockSpec((1, tn), lambda j: (0, j))],
        out_specs=pl.BlockSpec((SUBLANES, tn), lambda j: (0, j)),
        out_shape=jax.ShapeDtypeStruct((SUBLANES, n), F32),
        compiler_params=_params("parallel"),
        name="modulation",
    )(cc, w_mod, b_mod)


def _ffn_math(h, m, row0, g, wg_ref, wu_ref, wd_ref):
    shift, scale, gate = m[row0:row0 + 1], m[row0 + 1:row0 + 2], m[row0 + 2:row0 + 3]
    hn = (_rms(h, g) * (1.0 + scale) + shift).astype(BF16)
    a = jnp.dot(hn, wg_ref[...], preferred_element_type=F32)
    u = jnp.dot(hn, wu_ref[...], preferred_element_type=F32)
    act = (a * jax.nn.sigmoid(a) * u).astype(BF16)
    y = jnp.dot(act, wd_ref[...], preferred_element_type=F32)
    return h + 0.5 * gate * y


def _ffn1_kernel(x_ref, ctx_ref, mod_ref, g_ref, wg_ref, wu_ref, wd_ref, o_ref):
    is_ctx = pl.program_id(1) == 0
    h = jnp.where(is_ctx, ctx_ref[0], x_ref[0])
    o_ref[0] = _ffn_math(h, mod_ref[0, 0], 0, g_ref[...], wg_ref, wu_ref, wd_ref)


def _ffn1(x, ctx, mod, g, wg, wu, wd):
    bsz, seq, _ = x.shape
    nt = seq // TOK_TILE + 1
    tile = (1, TOK_TILE, D_MODEL)
    return pl.pallas_call(
        _ffn1_kernel,
        grid=(bsz, nt),
        in_specs=[pl.BlockSpec(tile, lambda b, t: (b, jnp.maximum(t - 1, 0), 0)),
                  pl.BlockSpec(tile, lambda b, t: (b, 0, 0)),
                  pl.BlockSpec((1, 1, N_MOD, D_MODEL), lambda b, t: (b, jnp.minimum(t, 1), 0, 0)),
                  _resident((1, D_MODEL)),
                  _resident((D_MODEL, D_FF)), _resident((D_MODEL, D_FF)), _resident((D_FF, D_MODEL))],
        out_specs=pl.BlockSpec(tile, lambda b, t: (b, t, 0)),
        out_shape=jax.ShapeDtypeStruct((bsz, nt * TOK_TILE, D_MODEL), F32),
        compiler_params=_params("parallel", "parallel"),
        name="ffn1",
    )(x, ctx, mod, g, wg, wu, wd)


def _ffn2_kernel(x_ref, mod_ref, g_ref, wg_ref, wu_ref, wd_ref, gf_ref, o_ref):
    y = _ffn_math(x_ref[0], mod_ref[0, 0], 6, g_ref[...], wg_ref, wu_ref, wd_ref)
    o_ref[0] = _rms(y, gf_ref[...])


def _ffn2(x, mod, g, wg, wu, wd, g_final):
    bsz, seq, _ = x.shape
    tile = (1, TOK_TILE, D_MODEL)
    return pl.pallas_call(
        _ffn2_kernel,
        grid=(bsz, seq // TOK_TILE),
        in_specs=[pl.BlockSpec(tile, lambda b, t: (b, t, 0)),
                  pl.BlockSpec((1, 1, N_MOD, D_MODEL), lambda b, t: (b, 1, 0, 0)),
                  _resident((1, D_MODEL)),
                  _resident((D_MODEL, D_FF)), _resident((D_MODEL, D_FF)), _resident((D_FF, D_MODEL)),
                  _resident((1, D_MODEL))],
        out_specs=pl.BlockSpec(tile, lambda b, t: (b, t, 0)),
        out_shape=jax.ShapeDtypeStruct((bsz, seq, D_MODEL), F32),
        compiler_params=_params("parallel", "parallel"),
        name="ffn2",
    )(x, mod, g, wg, wu, wd, g_final)


def _inproj_kernel(h_ref, mod_ref, g_ref, w_ref, prw_ref, plx_ref, plg_ref, pm_ref):
    m = mod_ref[0, 0]
    xn = (_rms(h_ref[0], g_ref[...]) * (1.0 + m[4:5]) + m[3:4]).astype(BF16)
    c0, c1, c2 = N_RW, N_RW + D_MODEL, N_RW + 2 * D_MODEL
    prw_ref[0] = jnp.dot(xn, w_ref[:, 0:c0], preferred_element_type=F32)
    plx_ref[0] = jnp.dot(xn, w_ref[:, c0:c1], preferred_element_type=F32)
    plg_ref[0] = jnp.dot(xn, w_ref[:, c1:c2], preferred_element_type=F32)
    pm_ref[0] = jnp.dot(xn, w_ref[:, c2:N_IN], preferred_element_type=F32)


def _inproj(h1, mod, g, w_in):
    bsz, tok, _ = h1.shape
    widths = (N_RW, D_MODEL, D_MODEL, 2 * D_MODEL)
    return pl.pallas_call(
        _inproj_kernel,
        grid=(bsz, tok // TOK_TILE),
        in_specs=[pl.BlockSpec((1, TOK_TILE, D_MODEL), lambda b, t: (b, t, 0)),
                  pl.BlockSpec((1, 1, N_MOD, D_MODEL), lambda b, t: (b, jnp.minimum(t, 1), 0, 0)),
                  _resident((1, D_MODEL)),
                  _resident((D_MODEL, N_IN))],
        out_specs=[pl.BlockSpec((1, TOK_TILE, w), lambda b, t: (b, t, 0)) for w in widths],
        out_shape=[jax.ShapeDtypeStruct((bsz, tok, w), F32) for w in widths],
        compiler_params=_params("parallel", "parallel"),
        name="inproj",
    )(h1, mod, g, w_in)


def _zshift_kernel(n_tiles, p_ref, up_ref, dn_ref, mu_ref, kk_w_ref, z_ref, kk_ref):
    t = pl.program_id(1)
    is_ctx = t == 0
    p = p_ref[0]
    row = lax.broadcasted_iota(jnp.int32, (TOK_TILE, 1), 0)
    period = jnp.where(is_ctx, TOK_TILE, GRID_W)
    pos = jnp.where(row >= period, row % GRID_W, row)
    first = pos == 0
    last = pos == period - 1
    left = jnp.where(first, 0.0, pltpu.roll(p, 1, 0))
    right = jnp.where(last, 0.0, pltpu.roll(p, TOK_TILE - 1, 0))
    up_halo = jnp.where(t > 1, up_ref[0], 0.0)
    dn_halo = jnp.where(t < n_tiles - 1, dn_ref[0], 0.0)
    up = jnp.concatenate([up_halo, p[:TOK_TILE - GRID_W]], axis=0)
    dn = jnp.concatenate([p[GRID_W:], dn_halo], axis=0)
    nb = jnp.where(is_ctx, 0.5 * (left + right), 0.25 * (up + dn + left + right))
    z = p + (nb - p) * mu_ref[...]
    z_ref[0] = z
    e, et = _head_sum_mats()
    kx = z[:, D_MODEL:2 * D_MODEL] * kk_w_ref[...]
    nrm = jnp.maximum(jnp.sqrt(_head_reduce(kx * kx, e)), 1e-12)
    kk_ref[0] = kx / _head_expand(nrm, et)


def _zshift(p_rw, mu, k_k):
    bsz, tok, _ = p_rw.shape
    nt = tok // TOK_TILE
    rows_per_tile = TOK_TILE // GRID_W
    n_rows = tok // GRID_W
    return pl.pallas_call(
        functools.partial(_zshift_kernel, nt),
        grid=(bsz, nt),
        in_specs=[pl.BlockSpec((1, TOK_TILE, N_RW), lambda b, t: (b, t, 0)),
                  pl.BlockSpec((1, GRID_W, N_RW),
                               lambda b, t: (b, jnp.maximum(t * rows_per_tile - 1, 0), 0)),
                  pl.BlockSpec((1, GRID_W, N_RW),
                               lambda b, t: (b, jnp.minimum((t + 1) * rows_per_tile, n_rows - 1), 0)),
                  pl.BlockSpec((1, N_RW), lambda b, t: (0, 0)),
                  pl.BlockSpec((1, D_MODEL), lambda b, t: (0, 0))],
        out_specs=[pl.BlockSpec((1, TOK_TILE, N_RW), lambda b, t: (b, t, 0)),
                   pl.BlockSpec((1, TOK_TILE, D_MODEL), lambda b, t: (b, t, 0))],
        out_shape=[jax.ShapeDtypeStruct((bsz, tok, N_RW), F32),
                   jax.ShapeDtypeStruct((bsz, tok, D_MODEL), F32)],
        compiler_params=_params("parallel", "parallel"),
        name="zshift",
    )(p_rw, p_rw, p_rw, mu, k_k)


def _scan_order(reverse, n_ctx, n_all, c):
    if not reverse:
        return c
    return jnp.where(c < n_ctx, n_ctx - 1 - c, n_all + n_ctx - 1 - c)


def _rwkv_kernel(reverse, z_ref, kk_ref, w0_ref, wup_ref, a0_ref, aup_ref, ka_ref, y_ref, st_ref):
    c = pl.program_id(1)

    @pl.when(c == 0)
    def _():
        st_ref[...] = jnp.zeros_like(st_ref)

    z = z_ref[0]
    lora = z[:, COL_LORA:COL_LORA + 256]
    lw = -DECAY_SCALE * jax.nn.sigmoid(w0_ref[0] + _dot(jnp.tanh(lora[:, :128]), wup_ref[0]))
    a = jax.nn.sigmoid(a0_ref[0] + _dot(lora[:, 128:], aup_ref[0]))

    ri = lax.broadcasted_iota(jnp.int32, (CHUNK, CHUNK), 0)
    ci = lax.broadcasted_iota(jnp.int32, (CHUNK, CHUNK), 1)
    tri = ((ci >= ri) if reverse else (ci <= ri)).astype(BF16)
    lw_hi, lw_lo = _split2(lw)
    cum = (jnp.dot(tri, lw_hi, preferred_element_type=F32)
           + jnp.dot(tri, lw_lo, preferred_element_type=F32))
    cum_last = cum[0:1] if reverse else cum[CHUNK - 1:CHUNK]

    r = z[:, 0:D_MODEL]
    k = z[:, D_MODEL:2 * D_MODEL]
    v = z[:, 2 * D_MODEL:3 * D_MODEL]
    kk = kk_ref[0]
    e_neg = jnp.exp(-cum)
    g_all = jnp.exp(cum_last)
    at_all = kk * jnp.exp(cum - lw)
    bt_all = (kk * a) * e_neg
    kt_all = (k * (1.0 + (a - 1.0) * ka_ref[...])) * e_neg
    rt_all = r * jnp.exp(cum)

    rr = lax.broadcasted_iota(jnp.int32, (PAIR, PAIR), 0)
    cc = lax.broadcasted_iota(jnp.int32, (PAIR, PAIR), 1)
    same = (rr // HEAD_DIM) == (cc // HEAD_DIM)
    tt, ss = rr % HEAD_DIM, cc % HEAD_DIM
    before = same & ((ss > tt) if reverse else (ss < tt))
    before_eq = same & ((ss >= tt) if reverse else (ss <= tt))
    eye = rr == cc
    lane = lax.broadcasted_iota(jnp.int32, (CHUNK, PAIR), 1)
    head0 = lane < HEAD_DIM

    def stack_masked(x):
        return jnp.concatenate([jnp.where(head0, x, 0.0), jnp.where(head0, 0.0, x)], axis=0)

    def stack(x):
        return jnp.concatenate([x, x], axis=0)

    pairs = range(N_HEADS // 2)
    sls = [slice(p * PAIR, (p + 1) * PAIR) for p in pairs]
    o = [_dot_nt(jnp.concatenate([stack_masked(at_all[:, s]), stack_masked(rt_all[:, s])], axis=0),
                 jnp.concatenate([stack(bt_all[:, s]), stack(kt_all[:, s])], axis=0)) for s in sls]
    a_ab = [jnp.where(before, x[:PAIR, :PAIR], 0.0) for x in o]
    a_ak = [jnp.where(before, x[:PAIR, PAIR:], 0.0) for x in o]
    rb = [jnp.where(before_eq, x[PAIR:, :PAIR], 0.0) for x in o]
    rk = [jnp.where(before_eq, x[PAIR:, PAIR:], 0.0) for x in o]
    inv = [jnp.where(eye, 1.0, 0.0) - x for x in a_ab]
    pw = [_dot(x, x) for x in a_ab]
    v_st = [stack(v[:, s]) for s in sls]
    akv = [_dot(a_ak[p], v_st[p]) for p in pairs]
    for _ in range(4):
        zz = [_dot(pw[p], jnp.concatenate([pw[p], inv[p]], axis=1)) for p in pairs]
        pw = [x[:, :PAIR] for x in zz]
        inv = [inv[p] + zz[p][:, PAIR:] for p in pairs]
    inv = [inv[p] + _dot(pw[p], inv[p]) for p in pairs]
    wu = [-_dot(inv[p], jnp.concatenate([stack(at_all[:, sls[p]]), akv[p]], axis=1)) for p in pairs]
    lhs = []
    for p in pairs:
        g = g_all[:, sls[p]]
        bh = stack_masked(bt_all[:, sls[p]] * g)
        kh = stack_masked(kt_all[:, sls[p]] * g)
        lhs.append(jnp.concatenate([jnp.concatenate([bh.T, kh.T], axis=1),
                                    jnp.concatenate([rb[p], rk[p]], axis=1)], axis=0))
    zero = jnp.zeros((PAIR, PAIR), F32)
    o = [_dot(lhs[p], jnp.concatenate([wu[p], jnp.concatenate([zero, v_st[p]], axis=1)], axis=0))
         for p in pairs]
    m_c = [jnp.where(same, o[p][:PAIR, :PAIR], 0.0) + jnp.where(eye, g_all[:, sls[p]], 0.0) for p in pairs]
    r_eff = [jnp.where(same, o[p][PAIR:, :PAIR] + stack(rt_all[:, sls[p]]), 0.0) for p in pairs]
    o2 = [_dot(jnp.concatenate([m_c[p], r_eff[p]], axis=0), st_ref[p]) for p in pairs]
    for p in pairs:
        st_ref[p] = o2[p][:PAIR] + jnp.where(same, o[p][:PAIR, PAIR:], 0.0)
        y_st = o2[p][PAIR:] + o[p][PAIR:, PAIR:]
        y_ref[0, :, sls[p]] = jnp.where(head0, y_st[:CHUNK], y_st[CHUNK:])


def _rwkv_scan(reverse, z, kk, w0, w_up_ext, a0, a_up_ext, k_a, n_ctx_chunks):
    bsz, tok, _ = z.shape
    n_all = tok // CHUNK
    order = functools.partial(_scan_order, reverse, n_ctx_chunks, n_all)
    return pl.pallas_call(
        functools.partial(_rwkv_kernel, reverse),
        grid=(bsz, n_all),
        in_specs=[pl.BlockSpec((1, CHUNK, N_RW), lambda b, c: (b, order(c), 0)),
                  pl.BlockSpec((1, CHUNK, D_MODEL), lambda b, c: (b, order(c), 0)),
                  pl.BlockSpec((1, 1, D_MODEL), lambda b, c: (0, 0, 0)),
                  pl.BlockSpec((1, 128, D_MODEL), lambda b, c: (0, 0, 0)),
                  pl.BlockSpec((1, 1, D_MODEL), lambda b, c: (0, 0, 0)),
                  pl.BlockSpec((1, 128, D_MODEL), lambda b, c: (0, 0, 0)),
                  pl.BlockSpec((1, D_MODEL), lambda b, c: (0, 0))],
        out_specs=pl.BlockSpec((1, CHUNK, D_MODEL), lambda b, c: (b, order(c), 0)),
        out_shape=jax.ShapeDtypeStruct((bsz, tok, D_MODEL), F32),
        scratch_shapes=[pltpu.VMEM((N_HEADS // 2, PAIR, PAIR), F32)],
        compiler_params=_params("parallel", "arbitrary"),
        name="rwkv_bwd" if reverse else "rwkv_fwd",
    )(z, kk, w0, w_up_ext, a0, a_up_ext, k_a)


def _lru_kernel(reverse, n_tiles, x_ref, prev_ref, next_ref, cw_ref, cb_ref, lam_ref,
                wa_ref, ba_ref, wx_ref, bx_ref, h_ref, carry_ref, a_scr, u_scr):
    t = pl.program_id(1)

    @pl.when(t == 0)
    def _():
        carry_ref[...] = jnp.zeros_like(carry_ref)

    ti = _scan_order(reverse, 1, n_tiles, t)
    prev = jnp.where(ti >= 2, prev_ref[0], 0.0)
    nxt = jnp.where((ti >= 1) & (ti <= n_tiles - 2), next_ref[0], 0.0)
    x = x_ref[0]
    ext = jnp.concatenate([prev, x, nxt], axis=0)
    n_ext = TOK_TILE + 2 * SUBLANES
    body = slice(SUBLANES, SUBLANES + TOK_TILE)
    cw = cw_ref[...]
    xc = (cw[0:1] * pltpu.roll(ext, 1, 0)[body] + cw[1:2] * x
          + cw[2:3] * pltpu.roll(ext, n_ext - 1, 0)[body]
          + cw[3:4] * pltpu.roll(ext, n_ext - 2, 0)[body] + cb_ref[...])
    xb = xc.astype(BF16)
    gr = jnp.concatenate([jnp.dot(xb[:, n * LRU_BLOCK:(n + 1) * LRU_BLOCK], wa_ref[0, n],
                                  preferred_element_type=F32) for n in range(LRU_BLOCKS)], axis=1)
    gi = jnp.concatenate([jnp.dot(xb[:, n * LRU_BLOCK:(n + 1) * LRU_BLOCK], wx_ref[0, n],
                                  preferred_element_type=F32) for n in range(LRU_BLOCKS)], axis=1)
    gate_r = jax.nn.sigmoid(gr + ba_ref[0])
    gate_i = jax.nn.sigmoid(gi + bx_ref[0])
    lam = lam_ref[0]
    log_sig = jnp.minimum(lam, 0.0) - jnp.log1p(jnp.exp(-jnp.abs(lam)))
    log_a = LRU_C * gate_r * log_sig
    a = jnp.exp(log_a)
    a_scr[...] = a
    u_scr[...] = jnp.sqrt(-jnp.tanh(log_a) * (a * a + 1.0)) * (gate_i * xc)

    row = lax.broadcasted_iota(jnp.int32, (SUBLANES, D_MODEL), 0)
    n_groups = TOK_TILE // SUBLANES

    def group(i, h_in):
        gidx = (n_groups - 1 - i) if reverse else i
        rows = pl.ds(pl.multiple_of(gidx * SUBLANES, SUBLANES), SUBLANES)
        a = a_scr[rows, :]
        u = u_scr[rows, :]
        for s in (1, 2, 4):
            if reverse:
                ok = row < SUBLANES - s
                sh = SUBLANES - s
            else:
                ok = row >= s
                sh = s
            u = jnp.where(ok, a * pltpu.roll(u, sh, 0) + u, u)
            a = jnp.where(ok, a * pltpu.roll(a, sh, 0), a)
        h = u + a * h_in
        h_ref[0, rows, :] = h
        last = h[0:1] if reverse else h[SUBLANES - 1:SUBLANES]
        return jnp.broadcast_to(last, (SUBLANES, D_MODEL))

    carry_ref[...] = lax.fori_loop(0, n_groups, group, carry_ref[...])


def _lru_scan(reverse, d, p_lx, conv_w, conv_b, lam, wa, ba, wx, bx):
    bsz, tok, _ = p_lx.shape
    nt = tok // TOK_TILE
    gpt = TOK_TILE // SUBLANES
    n_groups = tok // SUBLANES
    order = functools.partial(_scan_order, reverse, 1, nt)
    dsel = lambda *_: (d, 0, 0)
    return pl.pallas_call(
        functools.partial(_lru_kernel, reverse, nt),
        grid=(bsz, nt),
        in_specs=[pl.BlockSpec((1, TOK_TILE, D_MODEL), lambda b, t: (b, order(t), 0)),
                  pl.BlockSpec((1, SUBLANES, D_MODEL),
                               lambda b, t: (b, jnp.maximum(order(t) * gpt - 1, 0), 0)),
                  pl.BlockSpec((1, SUBLANES, D_MODEL),
                               lambda b, t: (b, jnp.minimum((order(t) + 1) * gpt, n_groups - 1), 0)),
                  pl.BlockSpec((4, D_MODEL), lambda b, t: (0, 0)),
                  pl.BlockSpec((1, D_MODEL), lambda b, t: (0, 0)),
                  pl.BlockSpec((1, 1, D_MODEL), dsel),
                  pl.BlockSpec((1, LRU_BLOCKS, LRU_BLOCK, LRU_BLOCK), lambda b, t: (d, 0, 0, 0)),
                  pl.BlockSpec((1, 1, D_MODEL), dsel),
                  pl.BlockSpec((1, LRU_BLOCKS, LRU_BLOCK, LRU_BLOCK), lambda b, t: (d, 0, 0, 0)),
                  pl.BlockSpec((1, 1, D_MODEL), dsel)],
        out_specs=pl.BlockSpec((1, TOK_TILE, D_MODEL), lambda b, t: (b, order(t), 0)),
        out_shape=jax.ShapeDtypeStruct((bsz, tok, D_MODEL), F32),
        scratch_shapes=[pltpu.VMEM((SUBLANES, D_MODEL), F32),
                        pltpu.VMEM((TOK_TILE, D_MODEL), F32),
                        pltpu.VMEM((TOK_TILE, D_MODEL), F32)],
        compiler_params=_params("parallel", "arbitrary"),
        name="lru_bwd" if reverse else "lru_fwd",
    )(p_lx, p_lx, p_lx, conv_w, conv_b, lam, wa, ba, wx, bx)


def _mixout_kernel(yf_ref, yb_ref, z_ref, hf_ref, hb_ref, plg_ref, pm_ref, h1_ref, mod_ref,
                   rk_ref, lnw_ref, lnb_ref, gup_ref, wprw_ref, wplru_ref, wout_ref, o_ref):
    e, et = _head_sum_mats()
    inv_n = 1.0 / HEAD_DIM
    y = yf_ref[0] + yb_ref[0]
    mu = _head_expand(_head_reduce(y, e) * inv_n, et)
    yc = y - mu
    var = _head_expand(_head_reduce(yc * yc, e) * inv_n, et)
    yn = yc * lax.rsqrt(var + LN_X_EPS) * lnw_ref[...] + lnb_ref[...]
    z = z_ref[0]
    r = z[:, 0:D_MODEL]
    k = z[:, D_MODEL:2 * D_MODEL]
    v = z[:, 2 * D_MODEL:3 * D_MODEL]
    bonus = _head_expand(_head_reduce(r * k * rk_ref[...], e), et) * v
    gate = _dot(jax.nn.sigmoid(z[:, COL_G:COL_G + 128]), gup_ref[...])
    y_rw = (yn + bonus) * gate
    lg = plg_ref[0]
    gelu = lg * (0.5 * (1.0 + jnp.tanh(math.sqrt(2.0 / math.pi) * (lg + 0.044715 * (lg * lg * lg)))))
    y_lru = (hf_ref[0] + hb_ref[0]) * gelu
    pm = pm_ref[0]
    merged = (jax.nn.sigmoid(pm[:, :D_MODEL]) * _dot(y_rw, wprw_ref[...])
              + jax.nn.sigmoid(pm[:, D_MODEL:]) * _dot(y_lru, wplru_ref[...]))
    m = mod_ref[0, 0]
    o_ref[0] = h1_ref[0] + m[5:6] * _dot(merged, wout_ref[...])


def _mixout(y_f, y_b, z, h_f, h_b, p_lg, p_m, h1, mod, r_k, ln_w, ln_b, g_up, w_proj_rw, w_proj_lru, w_out):
    bsz, tok, _ = h1.shape
    nt = tok // TOK_TILE - 1
    lat = lambda w: pl.BlockSpec((1, TOK_TILE, w), lambda b, t: (b, t + 1, 0))
    row = lambda: pl.BlockSpec((1, D_MODEL), lambda b, t: (0, 0))
    return pl.pallas_call(
        _mixout_kernel,
        grid=(bsz, nt),
        in_specs=[lat(D_MODEL), lat(D_MODEL), lat(N_RW), lat(D_MODEL), lat(D_MODEL), lat(D_MODEL),
                  lat(2 * D_MODEL), lat(D_MODEL),
                  pl.BlockSpec((1, 1, N_MOD, D_MODEL), lambda b, t: (b, 1, 0, 0)),
                  row(), row(), row(),
                  _resident((128, D_MODEL)), _resident((D_MODEL, D_MODEL)),
                  _resident((D_MODEL, D_MODEL)), _resident((D_MODEL, D_MODEL))],
        out_specs=pl.BlockSpec((1, TOK_TILE, D_MODEL), lambda b, t: (b, t, 0)),
        out_shape=jax.ShapeDtypeStruct((bsz, nt * TOK_TILE, D_MODEL), F32),
        compiler_params=_params("parallel", "parallel"),
        name="mixout",
    )(y_f, y_b, z, h_f, h_b, p_lg, p_m, h1, mod, r_k, ln_w, ln_b, g_up, w_proj_rw, w_proj_lru, w_out)


def _lora_ext(w_up, d):
    zero = jnp.zeros_like(w_up[0])
    parts = [w_up[0], zero] if d == 0 else [zero, w_up[1]]
    return jnp.concatenate(parts, axis=0)[None].astype(BF16)


def kernel(x, c, ctx, c_ctx, w_mod, b_mod, g_ffn1, ffn1_wg, ffn1_wu, ffn1_wd, g_mix, w_in, rw_mu, rw_w0, rw_w_up, rw_a0, rw_a_up, rw_g_up, rw_k_k, rw_k_a, rw_r_k, rw_ln_w, rw_ln_b, w_proj_rw, lru_conv_w, lru_conv_b, lru_lam, lru_wa, lru_ba, lru_wx, lru_bx, w_proj_lru, w_out, g_ffn2, ffn2_wg, ffn2_wu, ffn2_wd, g_final):
    bsz, seq, d_model = x.shape
    assert d_model == D_MODEL and seq % TOK_TILE == 0 and ctx.shape[1] == TOK_TILE
    assert w_mod.shape[0] == 1 and bsz < SUBLANES
    bf = lambda w: w.astype(BF16)
    row = lambda p: p.reshape(1, -1)

    cc = jnp.zeros((SUBLANES, D_MODEL), F32).at[:bsz].set(c).at[bsz].set(c_ctx)
    mod = _modulation(cc, w_mod[0], row(b_mod[0])).reshape(SUBLANES, N_MOD, D_MODEL)
    mod = jnp.stack([jnp.broadcast_to(mod[bsz], (bsz, N_MOD, D_MODEL)), mod[:bsz]], axis=1)

    h1 = _ffn1(x, ctx, mod, row(g_ffn1[0]), bf(ffn1_wg[0]), bf(ffn1_wu[0]), bf(ffn1_wd[0]))
    p_rw, p_lx, p_lg, p_m = _inproj(h1, mod, row(g_mix[0]), bf(w_in[0]))
    z, kk = _zshift(p_rw, row(rw_mu[0]), row(rw_k_k[0]))

    n_ctx_chunks = TOK_TILE // CHUNK
    ys, hs = [], []
    for d, reverse in ((0, False), (1, True)):
        ys.append(_rwkv_scan(reverse, z, kk, rw_w0[0, d].reshape(1, 1, -1), _lora_ext(rw_w_up[0], d),
                             rw_a0[0, d].reshape(1, 1, -1), _lora_ext(rw_a_up[0], d), row(rw_k_a[0]),
                             n_ctx_chunks))
        hs.append(_lru_scan(reverse, d, p_lx, lru_conv_w[0], row(lru_conv_b[0]),
                            lru_lam[0][:, None, :], bf(lru_wa[0]), lru_ba[0][:, None, :],
                            bf(lru_wx[0]), lru_bx[0][:, None, :]))

    x2 = _mixout(ys[0], ys[1], z, hs[0], hs[1], p_lg, p_m, h1, mod, row(rw_r_k[0]), row(rw_ln_w[0]),
                 row(rw_ln_b[0]), bf(rw_g_up[0]), bf(w_proj_rw[0]), bf(w_proj_lru[0]), bf(w_out[0]))
    return _ffn2(x2, mod, row(g_ffn2[0]), bf(ffn2_wg[0]), bf(ffn2_wu[0]), bf(ffn2_wd[0]), row(g_final))
```

```python
import functools
import math

import jax
import jax.numpy as jnp
from jax import lax
from jax.experimental import pallas as pl
from jax.experimental.pallas import tpu as pltpu

F32 = jnp.float32
BF16 = jnp.bfloat16

D_MODEL = 1024
D_FF = 2816
N_MOD = 9
NORM_EPS = 1e-6
HEAD_DIM = 64
N_HEADS = D_MODEL // HEAD_DIM
LN_X_EPS = 64e-5
DECAY_SCALE = math.exp(-0.5)
LRU_BLOCKS = 4
LRU_BLOCK = D_MODEL // LRU_BLOCKS
LRU_C = 8.0
GRID_W = 64
N_RW = 3 * D_MODEL + 4 * 64 + 128
COL_LORA = 3 * D_MODEL
COL_G = COL_LORA + 256
N_IN = N_RW + 4 * D_MODEL

TOK_TILE = 256
CHUNK = 64
PAIR = 2 * HEAD_DIM
RWKV_ROWS = 2
SHIFT_COLS = N_RW // 3
SUBLANES = 8
VMEM_LIMIT = 56 * 1024 * 1024


def _params(*sem):
    return pltpu.CompilerParams(dimension_semantics=sem, vmem_limit_bytes=VMEM_LIMIT)


def _resident(shape):
    nd = len(shape)
    return pl.BlockSpec(shape, lambda *_: (0,) * nd, pipeline_mode=pl.Buffered(1))


def _dot(a, b):
    return jnp.dot(a.astype(BF16), b.astype(BF16), preferred_element_type=F32)


def _dot_nt(a, b):
    return lax.dot_general(a.astype(BF16), b.astype(BF16), (((1,), (1,)), ((), ())),
                           preferred_element_type=F32)


def _split2(x):
    hi = x.astype(BF16)
    lo = (x - hi.astype(F32)).astype(BF16)
    return hi, lo


def _split3(x):
    hi = x.astype(BF16)
    r1 = x - hi.astype(F32)
    mid = r1.astype(BF16)
    lo = (r1 - mid.astype(F32)).astype(BF16)
    return hi, mid, lo


def _rms(x, g):
    return x * lax.rsqrt(jnp.mean(x * x, axis=-1, keepdims=True) + NORM_EPS) * g


def _head_sum_mats():
    c = lax.broadcasted_iota(jnp.int32, (D_MODEL, 128), 0) // HEAD_DIM
    j = lax.broadcasted_iota(jnp.int32, (D_MODEL, 128), 1)
    e = ((c == j % N_HEADS) & (j < 3 * N_HEADS)).astype(BF16)
    jt = lax.broadcasted_iota(jnp.int32, (128, D_MODEL), 0)
    ct = lax.broadcasted_iota(jnp.int32, (128, D_MODEL), 1) // HEAD_DIM
    et = ((ct == jt % N_HEADS) & (jt < 3 * N_HEADS)).astype(BF16)
    return e, et


def _head_reduce(x, e):
    hi, lo = _split2(x)
    return (jnp.dot(hi, e, preferred_element_type=F32)
            + jnp.dot(lo, e, preferred_element_type=F32))


def _head_expand(s, et):
    hi, mid, lo = _split3(s)
    lane = lax.broadcasted_iota(jnp.int32, s.shape, 1)
    parts = jnp.where(lane < N_HEADS, hi, jnp.where(lane < 2 * N_HEADS, mid, lo))
    return jnp.dot(parts, et, preferred_element_type=F32)


def _mod_kernel(c_ref, w_ref, b_ref, o_ref):
    c = c_ref[...]
    s = c * jax.nn.sigmoid(c)
    o_ref[...] = _dot(s, w_ref[...]) + b_ref[...]


def _modulation(cc, w_mod, b_mod):
    n = w_mod.shape[1]
    tn = 1152
    return pl.pallas_call(
        _mod_kernel,
        grid=(n // tn,),
        in_specs=[pl.BlockSpec((SUBLANES, D_MODEL), lambda j: (0, 0)),
                  pl.BlockSpec((D_MODEL, tn), lambda j: (0, j)),
                  pl.BlockSpec((1, tn), lambda j: (0, j))],
        out_specs=pl.BlockSpec((SUBLANES, tn), lambda j: (0, j)),
        out_shape=jax.ShapeDtypeStruct((SUBLANES, n), F32),
        compiler_params=_params("parallel"),
        name="modulation",
    )(cc, w_mod, b_mod)


def _ffn_math(h, m, row0, g, wg_ref, wu_ref, wd_ref):
    shift, scale, gate = m[row0:row0 + 1], m[row0 + 1:row0 + 2], m[row0 + 2:row0 + 3]
    hn = (_rms(h, g) * (1.0 + scale) + shift).astype(BF16)
    a = jnp.dot(hn, wg_ref[...], preferred_element_type=F32)
    u = jnp.dot(hn, wu_ref[...], preferred_element_type=F32)
    act = (a * jax.nn.sigmoid(a) * u).astype(BF16)
    y = jnp.dot(act, wd_ref[...], preferred_element_type=F32)
    return h + 0.5 * gate * y


def _ffn1_kernel(x_ref, ctx_ref, mod_ref, g_ref, wg_ref, wu_ref, wd_ref, o_ref):
    is_ctx = pl.program_id(1) == 0
    h = jnp.where(is_ctx, ctx_ref[0], x_ref[0])
    o_ref[0] = _ffn_math(h, mod_ref[0, 0], 0, g_ref[...], wg_ref, wu_ref, wd_ref)


def _ffn1(x, ctx, mod, g, wg, wu, wd):
    bsz, seq, _ = x.shape
    nt = seq // TOK_TILE + 1
    tile = (1, TOK_TILE, D_MODEL)
    return pl.pallas_call(
        _ffn1_kernel,
        grid=(bsz, nt),
        in_specs=[pl.BlockSpec(tile, lambda b, t: (b, jnp.maximum(t - 1, 0), 0)),
                  pl.BlockSpec(tile, lambda b, t: (b, 0, 0)),
                  pl.BlockSpec((1, 1, N_MOD, D_MODEL), lambda b, t: (b, jnp.minimum(t, 1), 0, 0)),
                  _resident((1, D_MODEL)),
                  _resident((D_MODEL, D_FF)), _resident((D_MODEL, D_FF)), _resident((D_FF, D_MODEL))],
        out_specs=pl.BlockSpec(tile, lambda b, t: (b, t, 0)),
        out_shape=jax.ShapeDtypeStruct((bsz, nt * TOK_TILE, D_MODEL), F32),
        compiler_params=_params("parallel", "parallel"),
        name="ffn1",
    )(x, ctx, mod, g, wg, wu, wd)


def _ffn2_kernel(x_ref, mod_ref, g_ref, wg_ref, wu_ref, wd_ref, gf_ref, o_ref):
    y = _ffn_math(x_ref[0], mod_ref[0, 0], 6, g_ref[...], wg_ref, wu_ref, wd_ref)
    o_ref[0] = _rms(y, gf_ref[...])


def _ffn2(x, mod, g, wg, wu, wd, g_final):
    bsz, seq, _ = x.shape
    tile = (1, TOK_TILE, D_MODEL)
    return pl.pallas_call(
        _ffn2_kernel,
        grid=(bsz, seq // TOK_TILE),
        in_specs=[pl.BlockSpec(tile, lambda b, t: (b, t, 0)),
                  pl.BlockSpec((1, 1, N_MOD, D_MODEL), lambda b, t: (b, 1, 0, 0)),
                  _resident((1, D_MODEL)),
                  _resident((D_MODEL, D_FF)), _resident((D_MODEL, D_FF)), _resident((D_FF, D_MODEL)),
                  _resident((1, D_MODEL))],
        out_specs=pl.BlockSpec(tile, lambda b, t: (b, t, 0)),
        out_shape=jax.ShapeDtypeStruct((bsz, seq, D_MODEL), F32),
        compiler_params=_params("parallel", "parallel"),
        name="ffn2",
    )(x, mod, g, wg, wu, wd, g_final)


def _inproj_kernel(n_tiles, h_ref, up_ref, dn_ref, mod_ref, g_ref, w_ref, mu_ref, kk_w_ref,
                   z_ref, kk_ref, plx_ref, plg_ref, pm_ref):
    t = pl.program_id(1)
    is_ctx = t == 0
    m = mod_ref[0, 0]
    hx = jnp.concatenate([up_ref[0], h_ref[0], dn_ref[0]], axis=0)
    xn = (_rms(hx, g_ref[...]) * (1.0 + m[4:5]) + m[3:4]).astype(BF16)
    xc = xn[GRID_W:GRID_W + TOK_TILE]
    c0, c1, c2 = N_RW, N_RW + D_MODEL, N_RW + 2 * D_MODEL
    plx_ref[0] = jnp.dot(xc, w_ref[:, c0:c1], preferred_element_type=F32)
    plg_ref[0] = jnp.dot(xc, w_ref[:, c1:c2], preferred_element_type=F32)
    pm_ref[0] = jnp.dot(xc, w_ref[:, c2:N_IN], preferred_element_type=F32)

    row = lax.broadcasted_iota(jnp.int32, (TOK_TILE, 1), 0)
    period = jnp.where(is_ctx, TOK_TILE, GRID_W)
    pos = jnp.where(row >= period, row % GRID_W, row)
    first = pos == 0
    last = pos == period - 1
    no_up = ((row < GRID_W) & (t <= 1)) | is_ctx
    no_dn = ((row >= TOK_TILE - GRID_W) & (t == n_tiles - 1)) | is_ctx
    nb_scale = jnp.where(is_ctx, 0.5, 0.25)
    for j in range(N_RW // SHIFT_COLS):
        cols = slice(j * SHIFT_COLS, (j + 1) * SHIFT_COLS)
        pe = jnp.dot(xn, w_ref[:, cols], preferred_element_type=F32)
        p = pe[GRID_W:GRID_W + TOK_TILE]
        left = jnp.where(first, 0.0, pltpu.roll(p, 1, 0))
        right = jnp.where(last, 0.0, pltpu.roll(p, TOK_TILE - 1, 0))
        up = jnp.where(no_up, 0.0, pe[:TOK_TILE])
        dn = jnp.where(no_dn, 0.0, pe[2 * GRID_W:])
        nb = nb_scale * (up + dn + left + right)
        z_ref[0, :, cols] = p + (nb - p) * mu_ref[:, cols]
    e, et = _head_sum_mats()
    kx = z_ref[0, :, D_MODEL:2 * D_MODEL] * kk_w_ref[...]
    nrm = jnp.maximum(jnp.sqrt(_head_reduce(kx * kx, e)), 1e-12)
    kk_ref[0] = kx / _head_expand(nrm, et)


def _inproj(h1, mod, g, w_in, mu, k_k):
    bsz, tok, _ = h1.shape
    nt = tok // TOK_TILE
    rows_per_tile = TOK_TILE // GRID_W
    n_rows = tok // GRID_W
    widths = (N_RW, D_MODEL, D_MODEL, D_MODEL, 2 * D_MODEL)
    return pl.pallas_call(
        functools.partial(_inproj_kernel, nt),
        grid=(bsz, nt),
        in_specs=[pl.BlockSpec((1, TOK_TILE, D_MODEL), lambda b, t: (b, t, 0)),
                  pl.BlockSpec((1, GRID_W, D_MODEL),
                               lambda b, t: (b, jnp.maximum(t * rows_per_tile - 1, 0), 0)),
                  pl.BlockSpec((1, GRID_W, D_MODEL),
                               lambda b, t: (b, jnp.minimum((t + 1) * rows_per_tile, n_rows - 1), 0)),
                  pl.BlockSpec((1, 1, N_MOD, D_MODEL), lambda b, t: (b, jnp.minimum(t, 1), 0, 0)),
                  _resident((1, D_MODEL)),
                  _resident((D_MODEL, N_IN)),
                  _resident((1, N_RW)),
                  _resident((1, D_MODEL))],
        out_specs=[pl.BlockSpec((1, TOK_TILE, w), lambda b, t: (b, t, 0)) for w in widths],
        out_shape=[jax.ShapeDtypeStruct((bsz, tok, w), F32) for w in widths],
        compiler_params=_params("parallel", "parallel"),
        name="inproj",
    )(h1, h1, h1, mod, g, w_in, mu, k_k)


def _scan_order(reverse, n_ctx, n_all, c):
    if not reverse:
        return c
    return jnp.where(c < n_ctx, n_ctx - 1 - c, n_all + n_ctx - 1 - c)


def _rwkv_kernel(reverse, z_ref, kk_ref, w0_ref, wup_ref, a0_ref, aup_ref, ka_ref, y_ref, st_ref):
    c = pl.program_id(1)

    @pl.when(c == 0)
    def _():
        st_ref[...] = jnp.zeros_like(st_ref)

    n_rows = z_ref.shape[0]
    n_pairs = N_HEADS // 2
    ri = lax.broadcasted_iota(jnp.int32, (CHUNK, CHUNK), 0)
    ci = lax.broadcasted_iota(jnp.int32, (CHUNK, CHUNK), 1)
    tri = ((ci >= ri) if reverse else (ci <= ri)).astype(BF16)
    at, bt, kt, rt, vv, gg = [], [], [], [], [], []
    for i in range(n_rows):
        z = z_ref[i]
        lora = z[:, COL_LORA:COL_LORA + 256]
        lw = -DECAY_SCALE * jax.nn.sigmoid(w0_ref[0] + _dot(jnp.tanh(lora[:, :128]), wup_ref[0]))
        a = jax.nn.sigmoid(a0_ref[0] + _dot(lora[:, 128:], aup_ref[0]))
        lw_hi, lw_lo = _split2(lw)
        cum = (jnp.dot(tri, lw_hi, preferred_element_type=F32)
               + jnp.dot(tri, lw_lo, preferred_element_type=F32))
        cum_last = cum[0:1] if reverse else cum[CHUNK - 1:CHUNK]
        r = z[:, 0:D_MODEL]
        k = z[:, D_MODEL:2 * D_MODEL]
        v = z[:, 2 * D_MODEL:3 * D_MODEL]
        kk = kk_ref[i]
        e_neg = jnp.exp(-cum)
        g_all = jnp.exp(cum_last)
        at_all = kk * jnp.exp(cum - lw)
        bt_all = (kk * a) * e_neg
        kt_all = (k * (1.0 + (a - 1.0) * ka_ref[...])) * e_neg
        rt_all = r * jnp.exp(cum)
        for p in range(n_pairs):
            s = slice(p * PAIR, (p + 1) * PAIR)
            at.append(at_all[:, s])
            bt.append(bt_all[:, s])
            kt.append(kt_all[:, s])
            rt.append(rt_all[:, s])
            vv.append(v[:, s])
            gg.append(g_all[:, s])

    tt = lax.broadcasted_iota(jnp.int32, (CHUNK, PAIR), 0)
    lane = lax.broadcasted_iota(jnp.int32, (CHUNK, PAIR), 1)
    ss = lane % HEAD_DIM
    before = (ss > tt) if reverse else (ss < tt)
    before_eq = (ss >= tt) if reverse else (ss <= tt)
    eye = jnp.where(ss == tt, 1.0, 0.0)
    head0 = lane < HEAD_DIM
    diag = (lax.broadcasted_iota(jnp.int32, (PAIR, PAIR), 0)
            == lax.broadcasted_iota(jnp.int32, (PAIR, PAIR), 1))

    def bd(x):
        xb = x.astype(BF16)
        zero = jnp.zeros_like(xb)
        return jnp.concatenate([jnp.where(head0, xb, zero), jnp.where(head0, zero, xb)], axis=0)

    pairs = range(n_rows * n_pairs)
    o = [_dot_nt(jnp.concatenate([at[p], rt[p]], axis=0),
                 jnp.concatenate([bd(bt[p]), bd(kt[p])], axis=0)) for p in pairs]
    a_ab = [jnp.where(before, x[:CHUNK, :PAIR], 0.0) for x in o]
    a_ak = [jnp.where(before, x[:CHUNK, PAIR:], 0.0) for x in o]
    rbk = [jnp.concatenate([jnp.where(before_eq, x[CHUNK:, :PAIR], 0.0),
                            jnp.where(before_eq, x[CHUNK:, PAIR:], 0.0)], axis=1).astype(BF16) for x in o]
    inv = [eye - x for x in a_ab]
    pw = [_dot(x, bd(x)) for x in a_ab]
    v_bd = [bd(x) for x in vv]
    akv = [_dot(a_ak[p], v_bd[p]) for p in pairs]
    for _ in range(4):
        zz = [_dot(pw[p], jnp.concatenate([bd(pw[p]), bd(inv[p])], axis=1)) for p in pairs]
        pw = [x[:, :PAIR] for x in zz]
        inv = [inv[p] + zz[p][:, PAIR:] for p in pairs]
    inv = [inv[p] + _dot(pw[p], bd(inv[p])) for p in pairs]
    wu = [-_dot(inv[p], jnp.concatenate([bd(at[p]), bd(akv[p])], axis=1)) for p in pairs]
    o = []
    for p in pairs:
        lhs = jnp.concatenate([
            rbk[p],
            jnp.concatenate([bd(bt[p] * gg[p]).T, bd(kt[p] * gg[p]).T], axis=1)], axis=0)
        rhs = jnp.concatenate([
            jnp.concatenate([bd(wu[p][:, :PAIR]), bd(wu[p][:, PAIR:])], axis=1),
            jnp.concatenate([jnp.zeros((PAIR, PAIR), BF16), v_bd[p]], axis=1)], axis=0)
        o.append(_dot(lhs, rhs))
    lhs2 = [jnp.concatenate([o[p][:CHUNK, :PAIR] + rt[p],
                             o[p][CHUNK:, :PAIR] + jnp.where(diag, gg[p], 0.0)], axis=0)
            for p in pairs]
    o2 = [_dot(lhs2[p], st_ref[p]) for p in pairs]
    for p in pairs:
        st_ref[p] = o2[p][CHUNK:] + o[p][CHUNK:, PAIR:]
        i, s = p // n_pairs, slice((p % n_pairs) * PAIR, (p % n_pairs + 1) * PAIR)
        y_ref[i, :, s] = o2[p][:CHUNK] + o[p][:CHUNK, PAIR:]


def _rwkv_scan(reverse, z, kk, w0, w_up_ext, a0, a_up_ext, k_a, n_ctx_chunks):
    bsz, tok, _ = z.shape
    n_all = tok // CHUNK
    rows = RWKV_ROWS if bsz % RWKV_ROWS == 0 else 1
    order = functools.partial(_scan_order, reverse, n_ctx_chunks, n_all)
    return pl.pallas_call(
        functools.partial(_rwkv_kernel, reverse),
        grid=(bsz // rows, n_all),
        in_specs=[pl.BlockSpec((rows, CHUNK, N_RW), lambda b, c: (b, order(c), 0)),
                  pl.BlockSpec((rows, CHUNK, D_MODEL), lambda b, c: (b, order(c), 0)),
                  pl.BlockSpec((1, 1, D_MODEL), lambda b, c: (0, 0, 0)),
                  pl.BlockSpec((1, 128, D_MODEL), lambda b, c: (0, 0, 0)),
                  pl.BlockSpec((1, 1, D_MODEL), lambda b, c: (0, 0, 0)),
                  pl.BlockSpec((1, 128, D_MODEL), lambda b, c: (0, 0, 0)),
                  pl.BlockSpec((1, D_MODEL), lambda b, c: (0, 0))],
        out_specs=pl.BlockSpec((rows, CHUNK, D_MODEL), lambda b, c: (b, order(c), 0)),
        out_shape=jax.ShapeDtypeStruct((bsz, tok, D_MODEL), F32),
        scratch_shapes=[pltpu.VMEM((rows * N_HEADS // 2, PAIR, PAIR), F32)],
        compiler_params=_params("parallel", "arbitrary"),
        name="rwkv_bwd" if reverse else "rwkv_fwd",
    )(z, kk, w0, w_up_ext, a0, a_up_ext, k_a)


def _lru_kernel(reverse, n_tiles, x_ref, prev_ref, next_ref, cw_ref, cb_ref, lam_ref,
                wa_ref, ba_ref, wx_ref, bx_ref, h_ref, carry_ref, a_scr, u_scr):
    t = pl.program_id(1)

    @pl.when(t == 0)
    def _():
        carry_ref[...] = jnp.zeros_like(carry_ref)

    ti = _scan_order(reverse, 1, n_tiles, t)
    prev = jnp.where(ti >= 2, prev_ref[0], 0.0)
    nxt = jnp.where((ti >= 1) & (ti <= n_tiles - 2), next_ref[0], 0.0)
    x = x_ref[0]
    ext = jnp.concatenate([prev, x, nxt], axis=0)
    n_ext = TOK_TILE + 2 * SUBLANES
    body = slice(SUBLANES, SUBLANES + TOK_TILE)
    cw = cw_ref[...]
    xc = (cw[0:1] * pltpu.roll(ext, 1, 0)[body] + cw[1:2] * x
          + cw[2:3] * pltpu.roll(ext, n_ext - 1, 0)[body]
          + cw[3:4] * pltpu.roll(ext, n_ext - 2, 0)[body] + cb_ref[...])
    xb = xc.astype(BF16)
    gr = jnp.concatenate([jnp.dot(xb[:, n * LRU_BLOCK:(n + 1) * LRU_BLOCK], wa_ref[0, n],
                                  preferred_element_type=F32) for n in range(LRU_BLOCKS)], axis=1)
    gi = jnp.concatenate([jnp.dot(xb[:, n * LRU_BLOCK:(n + 1) * LRU_BLOCK], wx_ref[0, n],
                                  preferred_element_type=F32) for n in range(LRU_BLOCKS)], axis=1)
    gate_r = jax.nn.sigmoid(gr + ba_ref[0])
    gate_i = jax.nn.sigmoid(gi + bx_ref[0])
    lam = lam_ref[0]
    log_sig = jnp.minimum(lam, 0.0) - jnp.log1p(jnp.exp(-jnp.abs(lam)))
    log_a = LRU_C * gate_r * log_sig
    a = jnp.exp(log_a)
    a_scr[...] = a
    u_scr[...] = jnp.sqrt(-jnp.tanh(log_a) * (a * a + 1.0)) * (gate_i * xc)

    row = lax.broadcasted_iota(jnp.int32, (SUBLANES, D_MODEL), 0)
    n_groups = TOK_TILE // SUBLANES

    def group(i, h_in):
        gidx = (n_groups - 1 - i) if reverse else i
        rows = pl.ds(pl.multiple_of(gidx * SUBLANES, SUBLANES), SUBLANES)
        a = a_scr[rows, :]
        u = u_scr[rows, :]
        for s in (1, 2, 4):
            if reverse:
                ok = row < SUBLANES - s
                sh = SUBLANES - s
            else:
                ok = row >= s
                sh = s
            u = jnp.where(ok, a * pltpu.roll(u, sh, 0) + u, u)
            a = jnp.where(ok, a * pltpu.roll(a, sh, 0), a)
        h = u + a * h_in
        h_ref[0, rows, :] = h
        last = h[0:1] if reverse else h[SUBLANES - 1:SUBLANES]
        return jnp.broadcast_to(last, (SUBLANES, D_MODEL))

    carry_ref[...] = lax.fori_loop(0, n_groups, group, carry_ref[...])


def _lru_scan(reverse, d, p_lx, conv_w, conv_b, lam, wa, ba, wx, bx):
    bsz, tok, _ = p_lx.shape
    nt = tok // TOK_TILE
    gpt = TOK_TILE // SUBLANES
    n_groups = tok // SUBLANES
    order = functools.partial(_scan_order, reverse, 1, nt)
    dsel = lambda *_: (d, 0, 0)
    return pl.pallas_call(
        functools.partial(_lru_kernel, reverse, nt),
        grid=(bsz, nt),
        in_specs=[pl.BlockSpec((1, TOK_TILE, D_MODEL), lambda b, t: (b, order(t), 0)),
                  pl.BlockSpec((1, SUBLANES, D_MODEL),
                               lambda b, t: (b, jnp.maximum(order(t) * gpt - 1, 0), 0)),
                  pl.BlockSpec((1, SUBLANES, D_MODEL),
                               lambda b, t: (b, jnp.minimum((order(t) + 1) * gpt, n_groups - 1), 0)),
                  pl.BlockSpec((4, D_MODEL), lambda b, t: (0, 0)),
                  pl.BlockSpec((1, D_MODEL), lambda b, t: (0, 0)),
                  pl.BlockSpec((1, 1, D_MODEL), dsel),
                  pl.BlockSpec((1, LRU_BLOCKS, LRU_BLOCK, LRU_BLOCK), lambda b, t: (d, 0, 0, 0)),
                  pl.BlockSpec((1, 1, D_MODEL), dsel),
                  pl.BlockSpec((1, LRU_BLOCKS, LRU_BLOCK, LRU_BLOCK), lambda b, t: (d, 0, 0, 0)),
                  pl.BlockSpec((1, 1, D_MODEL), dsel)],
        out_specs=pl.BlockSpec((1, TOK_TILE, D_MODEL), lambda b, t: (b, order(t), 0)),
        out_shape=jax.ShapeDtypeStruct((bsz, tok, D_MODEL), F32),
        scratch_shapes=[pltpu.VMEM((SUBLANES, D_MODEL), F32),
                        pltpu.VMEM((TOK_TILE, D_MODEL), F32),
                        pltpu.VMEM((TOK_TILE, D_MODEL), F32)],
        compiler_params=_params("parallel", "arbitrary"),
        name="lru_bwd" if reverse else "lru_fwd",
    )(p_lx, p_lx, p_lx, conv_w, conv_b, lam, wa, ba, wx, bx)


def _mixout_kernel(yf_ref, yb_ref, z_ref, hf_ref, hb_ref, plg_ref, pm_ref, h1_ref, mod_ref,
                   rk_ref, lnw_ref, lnb_ref, gup_ref, wprw_ref, wplru_ref, wout_ref, o_ref):
    e, et = _head_sum_mats()
    inv_n = 1.0 / HEAD_DIM
    y = yf_ref[0] + yb_ref[0]
    mu = _head_expand(_head_reduce(y, e) * inv_n, et)
    yc = y - mu
    var = _head_expand(_head_reduce(yc * yc, e) * inv_n, et)
    yn = yc * lax.rsqrt(var + LN_X_EPS) * lnw_ref[...] + lnb_ref[...]
    z = z_ref[0]
    r = z[:, 0:D_MODEL]
    k = z[:, D_MODEL:2 * D_MODEL]
    v = z[:, 2 * D_MODEL:3 * D_MODEL]
    bonus = _head_expand(_head_reduce(r * k * rk_ref[...], e), et) * v
    gate = _dot(jax.nn.sigmoid(z[:, COL_G:COL_G + 128]), gup_ref[...])
    y_rw = (yn + bonus) * gate
    lg = plg_ref[0]
    gelu = lg * (0.5 * (1.0 + jnp.tanh(math.sqrt(2.0 / math.pi) * (lg + 0.044715 * (lg * lg * lg)))))
    y_lru = (hf_ref[0] + hb_ref[0]) * gelu
    pm = pm_ref[0]
    merged = (jax.nn.sigmoid(pm[:, :D_MODEL]) * _dot(y_rw, wprw_ref[...])
              + jax.nn.sigmoid(pm[:, D_MODEL:]) * _dot(y_lru, wplru_ref[...]))
    m = mod_ref[0, 0]
    o_ref[0] = h1_ref[0] + m[5:6] * _dot(merged, wout_ref[...])


def _mixout(y_f, y_b, z, h_f, h_b, p_lg, p_m, h1, mod, r_k, ln_w, ln_b, g_up, w_proj_rw, w_proj_lru, w_out):
    bsz, tok, _ = h1.shape
    nt = tok // TOK_TILE - 1
    lat = lambda w: pl.BlockSpec((1, TOK_TILE, w), lambda b, t: (b, t + 1, 0))
    row = lambda: pl.BlockSpec((1, D_MODEL), lambda b, t: (0, 0))
    return pl.pallas_call(
        _mixout_kernel,
        grid=(bsz, nt),
        in_specs=[lat(D_MODEL), lat(D_MODEL), lat(N_RW), lat(D_MODEL), lat(D_MODEL), lat(D_MODEL),
                  lat(2 * D_MODEL), lat(D_MODEL),
                  pl.BlockSpec((1, 1, N_MOD, D_MODEL), lambda b, t: (b, 1, 0, 0)),
                  row(), row(), row(),
                  _resident((128, D_MODEL)), _resident((D_MODEL, D_MODEL)),
                  _resident((D_MODEL, D_MODEL)), _resident((D_MODEL, D_MODEL))],
        out_specs=pl.BlockSpec((1, TOK_TILE, D_MODEL), lambda b, t: (b, t, 0)),
        out_shape=jax.ShapeDtypeStruct((bsz, nt * TOK_TILE, D_MODEL), F32),
        compiler_params=_params("parallel", "parallel"),
        name="mixout",
    )(y_f, y_b, z, h_f, h_b, p_lg, p_m, h1, mod, r_k, ln_w, ln_b, g_up, w_proj_rw, w_proj_lru, w_out)


def _lora_ext(w_up, d):
    zero = jnp.zeros_like(w_up[0])
    parts = [w_up[0], zero] if d == 0 else [zero, w_up[1]]
    return jnp.concatenate(parts, axis=0)[None].astype(BF16)


def kernel(x, c, ctx, c_ctx, w_mod, b_mod, g_ffn1, ffn1_wg, ffn1_wu, ffn1_wd, g_mix, w_in, rw_mu, rw_w0, rw_w_up, rw_a0, rw_a_up, rw_g_up, rw_k_k, rw_k_a, rw_r_k, rw_ln_w, rw_ln_b, w_proj_rw, lru_conv_w, lru_conv_b, lru_lam, lru_wa, lru_ba, lru_wx, lru_bx, w_proj_lru, w_out, g_ffn2, ffn2_wg, ffn2_wu, ffn2_wd, g_final):
    bsz, seq, d_model = x.shape
    assert d_model == D_MODEL and seq % TOK_TILE == 0 and ctx.shape[1] == TOK_TILE
    assert w_mod.shape[0] == 1 and bsz < SUBLANES
    bf = lambda w: w.astype(BF16)
    row = lambda p: p.reshape(1, -1)

    cc = jnp.zeros((SUBLANES, D_MODEL), F32).at[:bsz].set(c).at[bsz].set(c_ctx)
    mod = _modulation(cc, w_mod[0], row(b_mod[0])).reshape(SUBLANES, N_MOD, D_MODEL)
    mod = jnp.stack([jnp.broadcast_to(mod[bsz], (bsz, N_MOD, D_MODEL)), mod[:bsz]], axis=1)

    h1 = _ffn1(x, ctx, mod, row(g_ffn1[0]), bf(ffn1_wg[0]), bf(ffn1_wu[0]), bf(ffn1_wd[0]))
    z, kk, p_lx, p_lg, p_m = _inproj(h1, mod, row(g_mix[0]), bf(w_in[0]), row(rw_mu[0]), row(rw_k_k[0]))

    n_ctx_chunks = TOK_TILE // CHUNK
    ys, hs = [], []
    for d, reverse in ((0, False), (1, True)):
        ys.append(_rwkv_scan(reverse, z, kk, rw_w0[0, d].reshape(1, 1, -1), _lora_ext(rw_w_up[0], d),
                             rw_a0[0, d].reshape(1, 1, -1), _lora_ext(rw_a_up[0], d), row(rw_k_a[0]),
                             n_ctx_chunks))
        hs.append(_lru_scan(reverse, d, p_lx, lru_conv_w[0], row(lru_conv_b[0]),
                            lru_lam[0][:, None, :], bf(lru_wa[0]), lru_ba[0][:, None, :],
                            bf(lru_wx[0]), lru_bx[0][:, None, :]))

    x2 = _mixout(ys[0], ys[1], z, hs[0], hs[1], p_lg, p_m, h1, mod, row(rw_r_k[0]), row(rw_ln_w[0]),
                 row(rw_ln_b[0]), bf(rw_g_up[0]), bf(w_proj_rw[0]), bf(w_proj_lru[0]), bf(w_out[0]))
    return _ffn2(x2, mod, row(g_ffn2[0]), bf(ffn2_wg[0]), bf(ffn2_wu[0]), bf(ffn2_wd[0]), row(g_final))
```

```python
import functools
import math

import jax
import jax.numpy as jnp
from jax import lax
from jax.experimental import pallas as pl
from jax.experimental.pallas import tpu as pltpu

F32 = jnp.float32
BF16 = jnp.bfloat16

D_MODEL = 1024
D_FF = 2816
N_MOD = 9
NORM_EPS = 1e-6
HEAD_DIM = 64
N_HEADS = D_MODEL // HEAD_DIM
LN_X_EPS = 64e-5
DECAY_SCALE = math.exp(-0.5)
LRU_BLOCKS = 4
LRU_BLOCK = D_MODEL // LRU_BLOCKS
LRU_C = 8.0
GRID_W = 64
N_RW = 3 * D_MODEL + 4 * 64 + 128
COL_LORA = 3 * D_MODEL
COL_G = COL_LORA + 256
N_IN = N_RW + 4 * D_MODEL

TOK_TILE = 256
CHUNK = 64
PAIR = 2 * HEAD_DIM
RWKV_ROWS = 2
RWKV_SUB = 4
PREP_COLS = 1024
MXU_COLS = 256
SHIFT_COL_STARTS = (0, 5 * MXU_COLS, 10 * MXU_COLS, N_RW)
SUBLANES = 8
VMEM_LIMIT = 56 * 1024 * 1024


def _params(*sem):
    return pltpu.CompilerParams(dimension_semantics=sem, vmem_limit_bytes=VMEM_LIMIT)


def _resident(shape):
    nd = len(shape)
    return pl.BlockSpec(shape, lambda *_: (0,) * nd, pipeline_mode=pl.Buffered(1))


def _dot(a, b):
    return jnp.dot(a.astype(BF16), b.astype(BF16), preferred_element_type=F32)


def _dot_nt(a, b):
    return lax.dot_general(a.astype(BF16), b.astype(BF16), (((1,), (1,)), ((), ())),
                           preferred_element_type=F32)


def _split2(x):
    hi = x.astype(BF16)
    lo = (x - hi.astype(F32)).astype(BF16)
    return hi, lo


def _split3(x):
    hi = x.astype(BF16)
    r1 = x - hi.astype(F32)
    mid = r1.astype(BF16)
    lo = (r1 - mid.astype(F32)).astype(BF16)
    return hi, mid, lo


def _rms(x, g):
    return x * lax.rsqrt(jnp.mean(x * x, axis=-1, keepdims=True) + NORM_EPS) * g


def _sigmoid(x):
    return 0.5 * jnp.tanh(0.5 * x) + 0.5


def _head_sum_mats():
    c = lax.broadcasted_iota(jnp.int32, (D_MODEL, 128), 0) // HEAD_DIM
    j = lax.broadcasted_iota(jnp.int32, (D_MODEL, 128), 1)
    e = ((c == j % N_HEADS) & (j < 3 * N_HEADS)).astype(BF16)
    jt = lax.broadcasted_iota(jnp.int32, (128, D_MODEL), 0)
    ct = lax.broadcasted_iota(jnp.int32, (128, D_MODEL), 1) // HEAD_DIM
    et = ((ct == jt % N_HEADS) & (jt < 3 * N_HEADS)).astype(BF16)
    return e, et


def _head_reduce(x, e):
    hi, lo = _split2(x)
    return (jnp.dot(hi, e, preferred_element_type=F32)
            + jnp.dot(lo, e, preferred_element_type=F32))


def _head_expand(s, et):
    hi, mid, lo = _split3(s)
    lane = lax.broadcasted_iota(jnp.int32, s.shape, 1)
    parts = jnp.where(lane < N_HEADS, hi, jnp.where(lane < 2 * N_HEADS, mid, lo))
    return jnp.dot(parts, et, preferred_element_type=F32)


def _mod_kernel(c_ref, w_ref, b_ref, o_ref):
    c = c_ref[...]
    s = c * _sigmoid(c)
    o_ref[...] = _dot(s, w_ref[...]) + b_ref[...]


def _modulation(cc, w_mod, b_mod):
    n = w_mod.shape[1]
    tn = 1152
    return pl.pallas_call(
        _mod_kernel,
        grid=(n // tn,),
        in_specs=[pl.BlockSpec((SUBLANES, D_MODEL), lambda j: (0, 0)),
                  pl.BlockSpec((D_MODEL, tn), lambda j: (0, j)),
                  pl.BlockSpec((1, tn), lambda j: (0, j))],
        out_specs=pl.BlockSpec((SUBLANES, tn), lambda j: (0, j)),
        out_shape=jax.ShapeDtypeStruct((SUBLANES, n), F32),
        compiler_params=_params("parallel"),
        name="modulation",
    )(cc, w_mod, b_mod)


def _ffn_math(h, m, row0, g, wg_ref, wu_ref, wd_ref):
    shift, scale, gate = m[row0:row0 + 1], m[row0 + 1:row0 + 2], m[row0 + 2:row0 + 3]
    hn = (_rms(h, g) * (1.0 + scale) + shift).astype(BF16)
    a = jnp.dot(hn, wg_ref[...], preferred_element_type=F32)
    u = jnp.dot(hn, wu_ref[...], preferred_element_type=F32)
    act = (a * _sigmoid(a) * u).astype(BF16)
    y = jnp.dot(act, wd_ref[...], preferred_element_type=F32)
    return h + 0.5 * gate * y


def _ffn1_kernel(x_ref, ctx_ref, mod_ref, g_ref, wg_ref, wu_ref, wd_ref, o_ref):
    is_ctx = pl.program_id(1) == 0
    h = jnp.where(is_ctx, ctx_ref[0], x_ref[0])
    o_ref[0] = _ffn_math(h, mod_ref[0, 0], 0, g_ref[...], wg_ref, wu_ref, wd_ref)


def _ffn1(x, ctx, mod, g, wg, wu, wd):
    bsz, seq, _ = x.shape
    nt = seq // TOK_TILE + 1
    tile = (1, TOK_TILE, D_MODEL)
    return pl.pallas_call(
        _ffn1_kernel,
        grid=(bsz, nt),
        in_specs=[pl.BlockSpec(tile, lambda b, t: (b, jnp.maximum(t - 1, 0), 0)),
                  pl.BlockSpec(tile, lambda b, t: (b, 0, 0)),
                  pl.BlockSpec((1, 1, N_MOD, D_MODEL), lambda b, t: (b, jnp.minimum(t, 1), 0, 0)),
                  _resident((1, D_MODEL)),
                  _resident((D_MODEL, D_FF)), _resident((D_MODEL, D_FF)), _resident((D_FF, D_MODEL))],
        out_specs=pl.BlockSpec(tile, lambda b, t: (b, t, 0)),
        out_shape=jax.ShapeDtypeStruct((bsz, nt * TOK_TILE, D_MODEL), F32),
        compiler_params=_params("parallel", "parallel"),
        name="ffn1",
    )(x, ctx, mod, g, wg, wu, wd)


def _ffn2_kernel(x_ref, mod_ref, g_ref, wg_ref, wu_ref, wd_ref, gf_ref, o_ref):
    y = _ffn_math(x_ref[0], mod_ref[0, 0], 6, g_ref[...], wg_ref, wu_ref, wd_ref)
    o_ref[0] = _rms(y, gf_ref[...])


def _ffn2(x, mod, g, wg, wu, wd, g_final):
    bsz, seq, _ = x.shape
    tile = (1, TOK_TILE, D_MODEL)
    return pl.pallas_call(
        _ffn2_kernel,
        grid=(bsz, seq // TOK_TILE),
        in_specs=[pl.BlockSpec(tile, lambda b, t: (b, t, 0)),
                  pl.BlockSpec((1, 1, N_MOD, D_MODEL), lambda b, t: (b, 1, 0, 0)),
                  _resident((1, D_MODEL)),
                  _resident((D_MODEL, D_FF)), _resident((D_MODEL, D_FF)), _resident((D_FF, D_MODEL)),
                  _resident((1, D_MODEL))],
        out_specs=pl.BlockSpec(tile, lambda b, t: (b, t, 0)),
        out_shape=jax.ShapeDtypeStruct((bsz, seq, D_MODEL), F32),
        compiler_params=_params("parallel", "parallel"),
        name="ffn2",
    )(x, mod, g, wg, wu, wd, g_final)


def _inproj_kernel(n_tiles, h_ref, up_ref, dn_ref, mod_ref, g_ref, w_ref, mu_ref, kk_w_ref,
                   z_ref, kk_ref, plx_ref, plg_ref, pm_ref):
    t = pl.program_id(1)
    is_ctx = t == 0
    m = mod_ref[0, 0]
    hx = jnp.concatenate([up_ref[0], h_ref[0], dn_ref[0]], axis=0)
    xn = (_rms(hx, g_ref[...]) * (1.0 + m[4:5]) + m[3:4]).astype(BF16)
    xc = xn[GRID_W:GRID_W + TOK_TILE]
    c0, c1, c2 = N_RW, N_RW + D_MODEL, N_RW + 2 * D_MODEL
    plain = [(plx_ref, slice(c0, c1)), (plg_ref, slice(c1, c2)), (pm_ref, slice(c2, N_IN))]

    row = lax.broadcasted_iota(jnp.int32, (TOK_TILE, 1), 0)
    period = jnp.where(is_ctx, TOK_TILE, GRID_W)
    pos = jnp.where(row >= period, row % GRID_W, row)
    first = pos == 0
    last = pos == period - 1
    no_up = ((row < GRID_W) & (t <= 1)) | is_ctx
    no_dn = ((row >= TOK_TILE - GRID_W) & (t == n_tiles - 1)) | is_ctx
    nb_scale = jnp.where(is_ctx, 0.5, 0.25)
    for j in range(len(plain)):
        cols = slice(SHIFT_COL_STARTS[j], SHIFT_COL_STARTS[j + 1])
        pe = jnp.dot(xn, w_ref[:, cols], preferred_element_type=F32)
        p = pe[GRID_W:GRID_W + TOK_TILE]
        left = jnp.where(first, 0.0, pltpu.roll(p, 1, 0))
        right = jnp.where(last, 0.0, pltpu.roll(p, TOK_TILE - 1, 0))
        up = jnp.where(no_up, 0.0, pe[:TOK_TILE])
        dn = jnp.where(no_dn, 0.0, pe[2 * GRID_W:])
        nb = nb_scale * (up + dn + left + right)
        z_ref[0, :, cols] = p + (nb - p) * mu_ref[:, cols]
        out_ref, wcols = plain[j]
        out_ref[0] = jnp.dot(xc, w_ref[:, wcols], preferred_element_type=F32)
    e, et = _head_sum_mats()
    kx = z_ref[0, :, D_MODEL:2 * D_MODEL] * kk_w_ref[...]
    nrm = jnp.maximum(jnp.sqrt(_head_reduce(kx * kx, e)), 1e-12)
    kk_ref[0] = kx / _head_expand(nrm, et)


def _inproj(h1, mod, g, w_in, mu, k_k):
    bsz, tok, _ = h1.shape
    nt = tok // TOK_TILE
    rows_per_tile = TOK_TILE // GRID_W
    n_rows = tok // GRID_W
    widths = (N_RW, D_MODEL, D_MODEL, D_MODEL, 2 * D_MODEL)
    return pl.pallas_call(
        functools.partial(_inproj_kernel, nt),
        grid=(bsz, nt),
        in_specs=[pl.BlockSpec((1, TOK_TILE, D_MODEL), lambda b, t: (b, t, 0)),
                  pl.BlockSpec((1, GRID_W, D_MODEL),
                               lambda b, t: (b, jnp.maximum(t * rows_per_tile - 1, 0), 0)),
                  pl.BlockSpec((1, GRID_W, D_MODEL),
                               lambda b, t: (b, jnp.minimum((t + 1) * rows_per_tile, n_rows - 1), 0)),
                  pl.BlockSpec((1, 1, N_MOD, D_MODEL), lambda b, t: (b, jnp.minimum(t, 1), 0, 0)),
                  _resident((1, D_MODEL)),
                  _resident((D_MODEL, N_IN)),
                  _resident((1, N_RW)),
                  _resident((1, D_MODEL))],
        out_specs=[pl.BlockSpec((1, TOK_TILE, w), lambda b, t: (b, t, 0)) for w in widths],
        out_shape=[jax.ShapeDtypeStruct((bsz, tok, w), F32) for w in widths],
        compiler_params=_params("parallel", "parallel"),
        name="inproj",
    )(h1, h1, h1, mod, g, w_in, mu, k_k)


def _scan_order(reverse, n_ctx, n_all, c):
    if not reverse:
        return c
    return jnp.where(c < n_ctx, n_ctx - 1 - c, n_all + n_ctx - 1 - c)


def _rwkv_kernel(reverse, z_ref, kk_ref, w0_ref, wup_ref, a0_ref, aup_ref, ka_ref, y_ref, st_ref):
    c = pl.program_id(1)

    @pl.when(c == 0)
    def _():
        st_ref[...] = jnp.zeros_like(st_ref)

    n_rows = z_ref.shape[0]
    n_sub = z_ref.shape[1] // CHUNK
    n_pairs = N_HEADS // 2
    ri = lax.broadcasted_iota(jnp.int32, (CHUNK, CHUNK), 0)
    ci = lax.broadcasted_iota(jnp.int32, (CHUNK, CHUNK), 1)
    tri = ((ci >= ri) if reverse else (ci <= ri)).astype(BF16)

    def prepare(i, rows, cols):
        col = lambda base: z_ref[i, rows, base + cols.start:base + cols.stop]
        lora = z_ref[i, rows, COL_LORA:COL_LORA + 256]
        lw = -DECAY_SCALE * _sigmoid(
            w0_ref[0, :, cols] + _dot(jnp.tanh(lora[:, :128]), wup_ref[0, :, cols]))
        a = _sigmoid(a0_ref[0, :, cols] + _dot(lora[:, 128:], aup_ref[0, :, cols]))
        lw_hi, lw_lo = _split2(lw)
        cum = (jnp.dot(tri, lw_hi, preferred_element_type=F32)
               + jnp.dot(tri, lw_lo, preferred_element_type=F32))
        cum_last = cum[0:1] if reverse else cum[CHUNK - 1:CHUNK]
        r, k, v = col(0), col(D_MODEL), col(2 * D_MODEL)
        kk = kk_ref[i, rows, cols]
        e_neg = jnp.exp(-cum)
        full = dict(at=kk * jnp.exp(cum - lw), bt=(kk * a) * e_neg,
                    kt=(k * (1.0 + (a - 1.0) * ka_ref[:, cols])) * e_neg,
                    rt=r * jnp.exp(cum), vv=v, gg=jnp.exp(cum_last))
        width = cols.stop - cols.start
        return [{name: x[:, s:s + PAIR] for name, x in full.items()} for s in range(0, width, PAIR)]

    tt = lax.broadcasted_iota(jnp.int32, (CHUNK, PAIR), 0)
    lane = lax.broadcasted_iota(jnp.int32, (CHUNK, PAIR), 1)
    ss = lane % HEAD_DIM
    before = (ss > tt) if reverse else (ss < tt)
    before_eq = (ss >= tt) if reverse else (ss <= tt)
    eye = jnp.where(ss == tt, 1.0, 0.0)
    head0 = lane < HEAD_DIM

    def bd(x):
        xb = x.astype(BF16)
        zero = jnp.zeros_like(xb)
        return jnp.concatenate([jnp.where(head0, xb, zero), jnp.where(head0, zero, xb)], axis=0)

    def tr(x):
        t = bd(x).T
        return jnp.where(head0, t[:CHUNK], t[CHUNK:])

    def solve(q, out):
        n = range(len(q))
        o = [_dot_nt(jnp.concatenate([x["at"], x["rt"]], axis=0),
                     jnp.concatenate([bd(x["bt"]), bd(x["kt"])], axis=0)) for x in q]
        a_ab = [jnp.where(before, x[:CHUNK, :PAIR], 0.0) for x in o]
        a_ak = [jnp.where(before, x[:CHUNK, PAIR:], 0.0) for x in o]
        rbk = [jnp.concatenate([jnp.where(before_eq, x[CHUNK:, :PAIR], 0.0),
                                jnp.where(before_eq, x[CHUNK:, PAIR:], 0.0)], axis=1).astype(BF16) for x in o]
        yield
        inv = [eye - x for x in a_ab]
        pw = [_dot(x, bd(x)) for x in a_ab]
        v_bd = [bd(x["vv"]) for x in q]
        akv = [_dot(a_ak[p], v_bd[p]) for p in n]
        yield
        for _ in range(4):
            zz = [_dot(pw[p], jnp.concatenate([bd(pw[p]), bd(inv[p])], axis=1)) for p in n]
            pw = [x[:, :PAIR] for x in zz]
            inv = [inv[p] + zz[p][:, PAIR:] for p in n]
            yield
        inv = [inv[p] + _dot(pw[p], bd(inv[p])) for p in n]
        yield
        wu = [-_dot(inv[p], jnp.concatenate([bd(q[p]["at"]), bd(akv[p])], axis=1)) for p in n]
        yield
        for p in n:
            g = q[p]["gg"]
            lhs = jnp.concatenate([
                rbk[p],
                jnp.concatenate([tr(q[p]["bt"] * g), tr(q[p]["kt"] * g)], axis=1)], axis=0)
            rhs = jnp.concatenate([
                jnp.concatenate([bd(wu[p][:, :PAIR]), bd(wu[p][:, PAIR:])], axis=1),
                jnp.concatenate([jnp.zeros((PAIR, PAIR), BF16), v_bd[p]], axis=1)], axis=0)
            o = _dot(lhs, rhs)
            out.append((jnp.concatenate([o[:CHUNK, :PAIR] + q[p]["rt"], o[CHUNK:, :PAIR] + eye * g], axis=0),
                        o[:CHUNK, PAIR:], o[CHUNK:, PAIR:]))

    subs = list(range(n_sub))[::-1] if reverse else list(range(n_sub))
    rows_of = {j: slice(j * CHUNK, (j + 1) * CHUNK) for j in subs}
    pieces = [(i, slice(c0, c0 + PREP_COLS)) for i in range(n_rows) for c0 in range(0, D_MODEL, PREP_COLS)]
    def carry(j, loc):
        o2 = [_dot(lhs2, bd(st_ref[p])) for p, (lhs2, _, _) in enumerate(loc)]
        for p, (_, y_loc, n_c) in enumerate(loc):
            st_ref[p] = o2[p][CHUNK:] + n_c
            i, s = p // n_pairs, slice((p % n_pairs) * PAIR, (p % n_pairs + 1) * PAIR)
            y_ref[i, rows_of[j], s] = o2[p][:CHUNK] + y_loc

    chains = [prepare(i, rows_of[subs[0]], cols) for i, cols in pieces]
    pending = None
    for idx, j in enumerate(subs):
        q = [x for part in chains for x in part]
        loc, chains = [], []
        todo = list(pieces) if idx + 1 < n_sub else []
        for _ in solve(q, loc):
            if pending is not None:
                carry(*pending)
                pending = None
            if todo:
                i, cols = todo.pop(0)
                chains.append(prepare(i, rows_of[subs[idx + 1]], cols))
        chains += [prepare(i, rows_of[subs[idx + 1]], cols) for i, cols in todo]
        pending = (j, loc)
    carry(*pending)


def _rwkv_scan(reverse, z, kk, w0, w_up_ext, a0, a_up_ext, k_a, n_ctx_chunks):
    bsz, tok, _ = z.shape
    rows = RWKV_ROWS if bsz % RWKV_ROWS == 0 else 1
    assert n_ctx_chunks % RWKV_SUB == 0 and (tok // CHUNK) % RWKV_SUB == 0
    blk = RWKV_SUB * CHUNK
    n_all = tok // blk
    order = functools.partial(_scan_order, reverse, n_ctx_chunks // RWKV_SUB, n_all)
    return pl.pallas_call(
        functools.partial(_rwkv_kernel, reverse),
        grid=(bsz // rows, n_all),
        in_specs=[pl.BlockSpec((rows, blk, N_RW), lambda b, c: (b, order(c), 0)),
                  pl.BlockSpec((rows, blk, D_MODEL), lambda b, c: (b, order(c), 0)),
                  pl.BlockSpec((1, 1, D_MODEL), lambda b, c: (0, 0, 0)),
                  pl.BlockSpec((1, 128, D_MODEL), lambda b, c: (0, 0, 0)),
                  pl.BlockSpec((1, 1, D_MODEL), lambda b, c: (0, 0, 0)),
                  pl.BlockSpec((1, 128, D_MODEL), lambda b, c: (0, 0, 0)),
                  pl.BlockSpec((1, D_MODEL), lambda b, c: (0, 0))],
        out_specs=pl.BlockSpec((rows, blk, D_MODEL), lambda b, c: (b, order(c), 0)),
        out_shape=jax.ShapeDtypeStruct((bsz, tok, D_MODEL), F32),
        scratch_shapes=[pltpu.VMEM((rows * N_HEADS // 2, CHUNK, PAIR), F32)],
        compiler_params=_params("parallel", "arbitrary"),
        name="rwkv_bwd" if reverse else "rwkv_fwd",
    )(z, kk, w0, w_up_ext, a0, a_up_ext, k_a)


def _lru_kernel(reverse, n_tiles, x_ref, prev_ref, next_ref, cw_ref, cb_ref, lam_ref,
                wa_ref, ba_ref, wx_ref, bx_ref, h_ref, carry_ref, a_scr, u_scr):
    t = pl.program_id(1)

    @pl.when(t == 0)
    def _():
        carry_ref[...] = jnp.zeros_like(carry_ref)

    ti = _scan_order(reverse, 1, n_tiles, t)
    prev = jnp.where(ti >= 2, prev_ref[0], 0.0)
    nxt = jnp.where((ti >= 1) & (ti <= n_tiles - 2), next_ref[0], 0.0)
    x = x_ref[0]
    ext = jnp.concatenate([prev, x, nxt], axis=0)
    n_ext = TOK_TILE + 2 * SUBLANES
    body = slice(SUBLANES, SUBLANES + TOK_TILE)
    cw = cw_ref[...]
    xc = (cw[0:1] * pltpu.roll(ext, 1, 0)[body] + cw[1:2] * x
          + cw[2:3] * pltpu.roll(ext, n_ext - 1, 0)[body]
          + cw[3:4] * pltpu.roll(ext, n_ext - 2, 0)[body] + cb_ref[...])
    xb = xc.astype(BF16)
    gr = jnp.concatenate([jnp.dot(xb[:, n * LRU_BLOCK:(n + 1) * LRU_BLOCK], wa_ref[0, n],
                                  preferred_element_type=F32) for n in range(LRU_BLOCKS)], axis=1)
    gi = jnp.concatenate([jnp.dot(xb[:, n * LRU_BLOCK:(n + 1) * LRU_BLOCK], wx_ref[0, n],
                                  preferred_element_type=F32) for n in range(LRU_BLOCKS)], axis=1)
    gate_r = _sigmoid(gr + ba_ref[0])
    gate_i = _sigmoid(gi + bx_ref[0])
    lam = lam_ref[0]
    log_sig = jnp.minimum(lam, 0.0) - jnp.log1p(jnp.exp(-jnp.abs(lam)))
    log_a = LRU_C * gate_r * log_sig
    a = jnp.exp(log_a)
    a_scr[...] = a
    u_scr[...] = jnp.sqrt(-jnp.tanh(log_a) * (a * a + 1.0)) * (gate_i * xc)

    row = lax.broadcasted_iota(jnp.int32, (SUBLANES, D_MODEL), 0)
    n_groups = TOK_TILE // SUBLANES

    def group(i, h_in):
        gidx = (n_groups - 1 - i) if reverse else i
        rows = pl.ds(pl.multiple_of(gidx * SUBLANES, SUBLANES), SUBLANES)
        a = a_scr[rows, :]
        u = u_scr[rows, :]
        for s in (1, 2, 4):
            if reverse:
                ok = row < SUBLANES - s
                sh = SUBLANES - s
            else:
                ok = row >= s
                sh = s
            u = jnp.where(ok, a * pltpu.roll(u, sh, 0) + u, u)
            a = jnp.where(ok, a * pltpu.roll(a, sh, 0), a)
        h = u + a * h_in
        h_ref[0, rows, :] = h
        last = h[0:1] if reverse else h[SUBLANES - 1:SUBLANES]
        return jnp.broadcast_to(last, (SUBLANES, D_MODEL))

    carry_ref[...] = lax.fori_loop(0, n_groups, group, carry_ref[...])


def _lru_scan(reverse, d, p_lx, conv_w, conv_b, lam, wa, ba, wx, bx):
    bsz, tok, _ = p_lx.shape
    nt = tok // TOK_TILE
    gpt = TOK_TILE // SUBLANES
    n_groups = tok // SUBLANES
    order = functools.partial(_scan_order, reverse, 1, nt)
    dsel = lambda *_: (d, 0, 0)
    return pl.pallas_call(
        functools.partial(_lru_kernel, reverse, nt),
        grid=(bsz, nt),
        in_specs=[pl.BlockSpec((1, TOK_TILE, D_MODEL), lambda b, t: (b, order(t), 0)),
                  pl.BlockSpec((1, SUBLANES, D_MODEL),
                               lambda b, t: (b, jnp.maximum(order(t) * gpt - 1, 0), 0)),
                  pl.BlockSpec((1, SUBLANES, D_MODEL),
                               lambda b, t: (b, jnp.minimum((order(t) + 1) * gpt, n_groups - 1), 0)),
                  pl.BlockSpec((4, D_MODEL), lambda b, t: (0, 0)),
                  pl.BlockSpec((1, D_MODEL), lambda b, t: (0, 0)),
                  pl.BlockSpec((1, 1, D_MODEL), dsel),
                  pl.BlockSpec((1, LRU_BLOCKS, LRU_BLOCK, LRU_BLOCK), lambda b, t: (d, 0, 0, 0)),
                  pl.BlockSpec((1, 1, D_MODEL), dsel),
                  pl.BlockSpec((1, LRU_BLOCKS, LRU_BLOCK, LRU_BLOCK), lambda b, t: (d, 0, 0, 0)),
                  pl.BlockSpec((1, 1, D_MODEL), dsel)],
        out_specs=pl.BlockSpec((1, TOK_TILE, D_MODEL), lambda b, t: (b, order(t), 0)),
        out_shape=jax.ShapeDtypeStruct((bsz, tok, D_MODEL), F32),
        scratch_shapes=[pltpu.VMEM((SUBLANES, D_MODEL), F32),
                        pltpu.VMEM((TOK_TILE, D_MODEL), F32),
                        pltpu.VMEM((TOK_TILE, D_MODEL), F32)],
        compiler_params=_params("parallel", "arbitrary"),
        name="lru_bwd" if reverse else "lru_fwd",
    )(p_lx, p_lx, p_lx, conv_w, conv_b, lam, wa, ba, wx, bx)


def _mixout_kernel(yf_ref, yb_ref, z_ref, hf_ref, hb_ref, plg_ref, pm_ref, h1_ref, mod_ref,
                   rk_ref, lnw_ref, lnb_ref, gup_ref, wprw_ref, wplru_ref, wout_ref, o_ref):
    e, et = _head_sum_mats()
    inv_n = 1.0 / HEAD_DIM
    y = yf_ref[0] + yb_ref[0]
    mu = _head_expand(_head_reduce(y, e) * inv_n, et)
    yc = y - mu
    var = _head_expand(_head_reduce(yc * yc, e) * inv_n, et)
    yn = yc * lax.rsqrt(var + LN_X_EPS) * lnw_ref[...] + lnb_ref[...]
    z = z_ref[0]
    r = z[:, 0:D_MODEL]
    k = z[:, D_MODEL:2 * D_MODEL]
    v = z[:, 2 * D_MODEL:3 * D_MODEL]
    bonus = _head_expand(_head_reduce(r * k * rk_ref[...], e), et) * v
    gate = _dot(_sigmoid(z[:, COL_G:COL_G + 128]), gup_ref[...])
    y_rw = (yn + bonus) * gate
    lg = plg_ref[0]
    gelu = lg * (0.5 * (1.0 + jnp.tanh(math.sqrt(2.0 / math.pi) * (lg + 0.044715 * (lg * lg * lg)))))
    y_lru = (hf_ref[0] + hb_ref[0]) * gelu
    pm = pm_ref[0]
    merged = (_sigmoid(pm[:, :D_MODEL]) * _dot(y_rw, wprw_ref[...])
              + _sigmoid(pm[:, D_MODEL:]) * _dot(y_lru, wplru_ref[...]))
    m = mod_ref[0, 0]
    o_ref[0] = h1_ref[0] + m[5:6] * _dot(merged, wout_ref[...])


def _mixout(y_f, y_b, z, h_f, h_b, p_lg, p_m, h1, mod, r_k, ln_w, ln_b, g_up, w_proj_rw, w_proj_lru, w_out):
    bsz, tok, _ = h1.shape
    nt = tok // TOK_TILE - 1
    lat = lambda w: pl.BlockSpec((1, TOK_TILE, w), lambda b, t: (b, t + 1, 0))
    row = lambda: pl.BlockSpec((1, D_MODEL), lambda b, t: (0, 0))
    return pl.pallas_call(
        _mixout_kernel,
        grid=(bsz, nt),
        in_specs=[lat(D_MODEL), lat(D_MODEL), lat(N_RW), lat(D_MODEL), lat(D_MODEL), lat(D_MODEL),
                  lat(2 * D_MODEL), lat(D_MODEL),
                  pl.BlockSpec((1, 1, N_MOD, D_MODEL), lambda b, t: (b, 1, 0, 0)),
                  row(), row(), row(),
                  _resident((128, D_MODEL)), _resident((D_MODEL, D_MODEL)),
                  _resident((D_MODEL, D_MODEL)), _resident((D_MODEL, D_MODEL))],
        out_specs=pl.BlockSpec((1, TOK_TILE, D_MODEL), lambda b, t: (b, t, 0)),
        out_shape=jax.ShapeDtypeStruct((bsz, nt * TOK_TILE, D_MODEL), F32),
        compiler_params=_params("parallel", "parallel"),
        name="mixout",
    )(y_f, y_b, z, h_f, h_b, p_lg, p_m, h1, mod, r_k, ln_w, ln_b, g_up, w_proj_rw, w_proj_lru, w_out)


def _lora_ext(w_up, d):
    zero = jnp.zeros_like(w_up[0])
    parts = [w_up[0], zero] if d == 0 else [zero, w_up[1]]
    return jnp.concatenate(parts, axis=0)[None].astype(BF16)


def kernel(x, c, ctx, c_ctx, w_mod, b_mod, g_ffn1, ffn1_wg, ffn1_wu, ffn1_wd, g_mix, w_in, rw_mu, rw_w0, rw_w_up, rw_a0, rw_a_up, rw_g_up, rw_k_k, rw_k_a, rw_r_k, rw_ln_w, rw_ln_b, w_proj_rw, lru_conv_w, lru_conv_b, lru_lam, lru_wa, lru_ba, lru_wx, lru_bx, w_proj_lru, w_out, g_ffn2, ffn2_wg, ffn2_wu, ffn2_wd, g_final):
    bsz, seq, d_model = x.shape
    assert d_model == D_MODEL and seq % TOK_TILE == 0 and ctx.shape[1] == TOK_TILE
    assert w_mod.shape[0] == 1 and bsz < SUBLANES
    bf = lambda w: w.astype(BF16)
    row = lambda p: p.reshape(1, -1)

    cc = jnp.zeros((SUBLANES, D_MODEL), F32).at[:bsz].set(c).at[bsz].set(c_ctx)
    mod = _modulation(cc, w_mod[0], row(b_mod[0])).reshape(SUBLANES, N_MOD, D_MODEL)
    mod = jnp.stack([jnp.broadcast_to(mod[bsz], (bsz, N_MOD, D_MODEL)), mod[:bsz]], axis=1)

    h1 = _ffn1(x, ctx, mod, row(g_ffn1[0]), bf(ffn1_wg[0]), bf(ffn1_wu[0]), bf(ffn1_wd[0]))
    z, kk, p_lx, p_lg, p_m = _inproj(h1, mod, row(g_mix[0]), bf(w_in[0]), row(rw_mu[0]), row(rw_k_k[0]))

    n_ctx_chunks = TOK_TILE // CHUNK
    ys, hs = [], []
    for d, reverse in ((0, False), (1, True)):
        ys.append(_rwkv_scan(reverse, z, kk, rw_w0[0, d].reshape(1, 1, -1), _lora_ext(rw_w_up[0], d),
                             rw_a0[0, d].reshape(1, 1, -1), _lora_ext(rw_a_up[0], d), row(rw_k_a[0]),
                             n_ctx_chunks))
        hs.append(_lru_scan(reverse, d, p_lx, lru_conv_w[0], row(lru_conv_b[0]),
                            lru_lam[0][:, None, :], bf(lru_wa[0]), lru_ba[0][:, None, :],
                            bf(lru_wx[0]), lru_bx[0][:, None, :]))

    x2 = _mixout(ys[0], ys[1], z, hs[0], hs[1], p_lg, p_m, h1, mod, row(rw_r_k[0]), row(rw_ln_w[0]),
                 row(rw_ln_b[0]), bf(rw_g_up[0]), bf(w_proj_rw[0]), bf(w_proj_lru[0]), bf(w_out[0]))
    return _ffn2(x2, mod, row(g_ffn2[0]), bf(ffn2_wg[0]), bf(ffn2_wu[0]), bf(ffn2_wd[0]), row(g_final))
```

```python
import functools
import math

import jax
import jax.numpy as jnp
from jax import lax
from jax.experimental import pallas as pl
from jax.experimental.pallas import tpu as pltpu

F32 = jnp.float32
BF16 = jnp.bfloat16

D_MODEL = 1024
D_FF = 2816
N_MOD = 9
NORM_EPS = 1e-6
HEAD_DIM = 64
N_HEADS = D_MODEL // HEAD_DIM
LN_X_EPS = 64e-5
DECAY_SCALE = math.exp(-0.5)
LRU_BLOCKS = 4
LRU_BLOCK = D_MODEL // LRU_BLOCKS
LRU_C = 8.0
GRID_W = 64
N_RW = 3 * D_MODEL + 4 * 64 + 128
COL_LORA = 3 * D_MODEL
COL_G = COL_LORA + 256
N_IN = N_RW + 4 * D_MODEL

TOK_TILE = 256
FFN2_TILE = 512
CHUNK = 64
PAIR = 2 * HEAD_DIM
RWKV_ROWS = 2
RWKV_SUB = 4
PREP_COLS = 1024
MXU_COLS = 256
SHIFT_COL_STARTS = (0, 5 * MXU_COLS, 10 * MXU_COLS, N_RW)
SUBLANES = 8
VMEM_LIMIT = 56 * 1024 * 1024


def _params(*sem):
    return pltpu.CompilerParams(dimension_semantics=sem, vmem_limit_bytes=VMEM_LIMIT)


def _resident(shape):
    nd = len(shape)
    return pl.BlockSpec(shape, lambda *_: (0,) * nd, pipeline_mode=pl.Buffered(1))


def _dot(a, b):
    return jnp.dot(a.astype(BF16), b.astype(BF16), preferred_element_type=F32)


def _dot_nt(a, b):
    return lax.dot_general(a.astype(BF16), b.astype(BF16), (((1,), (1,)), ((), ())),
                           preferred_element_type=F32)


def _split2(x):
    hi = x.astype(BF16)
    lo = (x - hi.astype(F32)).astype(BF16)
    return hi, lo


def _split3(x):
    hi = x.astype(BF16)
    r1 = x - hi.astype(F32)
    mid = r1.astype(BF16)
    lo = (r1 - mid.astype(F32)).astype(BF16)
    return hi, mid, lo


def _rms(x, g):
    return x * lax.rsqrt(jnp.mean(x * x, axis=-1, keepdims=True) + NORM_EPS) * g


def _sigmoid(x):
    return 0.5 * jnp.tanh(0.5 * x) + 0.5


def _head_sum_mats():
    c = lax.broadcasted_iota(jnp.int32, (D_MODEL, 128), 0) // HEAD_DIM
    j = lax.broadcasted_iota(jnp.int32, (D_MODEL, 128), 1)
    e = ((c == j % N_HEADS) & (j < 3 * N_HEADS)).astype(BF16)
    jt = lax.broadcasted_iota(jnp.int32, (128, D_MODEL), 0)
    ct = lax.broadcasted_iota(jnp.int32, (128, D_MODEL), 1) // HEAD_DIM
    et = ((ct == jt % N_HEADS) & (jt < 3 * N_HEADS)).astype(BF16)
    return e, et


def _head_reduce(x, e):
    hi, lo = _split2(x)
    return (jnp.dot(hi, e, preferred_element_type=F32)
            + jnp.dot(lo, e, preferred_element_type=F32))


def _head_expand(s, et):
    hi, mid, lo = _split3(s)
    lane = lax.broadcasted_iota(jnp.int32, s.shape, 1)
    parts = jnp.where(lane < N_HEADS, hi, jnp.where(lane < 2 * N_HEADS, mid, lo))
    return jnp.dot(parts, et, preferred_element_type=F32)


def _mod_kernel(c_ref, w_ref, b_ref, o_ref):
    c = c_ref[...]
    s = c * _sigmoid(c)
    o_ref[...] = _dot(s, w_ref[...]) + b_ref[...]


def _modulation(cc, w_mod, b_mod):
    n = w_mod.shape[1]
    tn = 1152
    return pl.pallas_call(
        _mod_kernel,
        grid=(n // tn,),
        in_specs=[pl.BlockSpec((SUBLANES, D_MODEL), lambda j: (0, 0)),
                  pl.BlockSpec((D_MODEL, tn), lambda j: (0, j)),
                  pl.BlockSpec((1, tn), lambda j: (0, j))],
        out_specs=pl.BlockSpec((SUBLANES, tn), lambda j: (0, j)),
        out_shape=jax.ShapeDtypeStruct((SUBLANES, n), F32),
        compiler_params=_params("parallel"),
        name="modulation",
    )(cc, w_mod, b_mod)


def _ffn_math(h, m, row0, g, wg_ref, wu_ref, wd_ref):
    shift, scale, gate = m[row0:row0 + 1], m[row0 + 1:row0 + 2], m[row0 + 2:row0 + 3]
    hn = (_rms(h, g) * (1.0 + scale) + shift).astype(BF16)
    a = jnp.dot(hn, wg_ref[...], preferred_element_type=F32)
    u = jnp.dot(hn, wu_ref[...], preferred_element_type=F32)
    act = (a * _sigmoid(a) * u).astype(BF16)
    y = jnp.dot(act, wd_ref[...], preferred_element_type=F32)
    return h + 0.5 * gate * y


def _ffn1_kernel(x_ref, ctx_ref, mod_ref, g_ref, wg_ref, wu_ref, wd_ref, o_ref):
    is_ctx = pl.program_id(1) == 0
    h = jnp.where(is_ctx, ctx_ref[0], x_ref[0])
    o_ref[0] = _ffn_math(h, mod_ref[0, 0], 0, g_ref[...], wg_ref, wu_ref, wd_ref)


def _ffn1(x, ctx, mod, g, wg, wu, wd):
    bsz, seq, _ = x.shape
    nt = seq // TOK_TILE + 1
    tile = (1, TOK_TILE, D_MODEL)
    return pl.pallas_call(
        _ffn1_kernel,
        grid=(bsz, nt),
        in_specs=[pl.BlockSpec(tile, lambda b, t: (b, jnp.maximum(t - 1, 0), 0)),
                  pl.BlockSpec(tile, lambda b, t: (b, 0, 0)),
                  pl.BlockSpec((1, 1, N_MOD, D_MODEL), lambda b, t: (b, jnp.minimum(t, 1), 0, 0)),
                  _resident((1, D_MODEL)),
                  _resident((D_MODEL, D_FF)), _resident((D_MODEL, D_FF)), _resident((D_FF, D_MODEL))],
        out_specs=pl.BlockSpec(tile, lambda b, t: (b, t, 0)),
        out_shape=jax.ShapeDtypeStruct((bsz, nt * TOK_TILE, D_MODEL), F32),
        compiler_params=_params("parallel", "parallel"),
        name="ffn1",
    )(x, ctx, mod, g, wg, wu, wd)


def _ffn2_kernel(x_ref, mod_ref, g_ref, wg_ref, wu_ref, wd_ref, gf_ref, o_ref):
    y = _ffn_math(x_ref[0], mod_ref[0, 0], 6, g_ref[...], wg_ref, wu_ref, wd_ref)
    o_ref[0] = _rms(y, gf_ref[...])


def _ffn2(x, mod, g, wg, wu, wd, g_final):
    bsz, seq, _ = x.shape
    tile = (1, FFN2_TILE, D_MODEL)
    return pl.pallas_call(
        _ffn2_kernel,
        grid=(bsz, seq // FFN2_TILE),
        in_specs=[pl.BlockSpec(tile, lambda b, t: (b, t, 0)),
                  pl.BlockSpec((1, 1, N_MOD, D_MODEL), lambda b, t: (b, 1, 0, 0)),
                  _resident((1, D_MODEL)),
                  _resident((D_MODEL, D_FF)), _resident((D_MODEL, D_FF)), _resident((D_FF, D_MODEL)),
                  _resident((1, D_MODEL))],
        out_specs=pl.BlockSpec(tile, lambda b, t: (b, t, 0)),
        out_shape=jax.ShapeDtypeStruct((bsz, seq, D_MODEL), F32),
        compiler_params=_params("parallel", "parallel"),
        name="ffn2",
    )(x, mod, g, wg, wu, wd, g_final)


def _inproj_kernel(n_tiles, h_ref, dn_ref, mod_ref, g_ref, w_ref, mu_ref, kk_w_ref,
                   z_ref, kk_ref, plx_ref, plg_ref, pm_ref, above_ref):
    t = pl.program_id(1)
    is_ctx = t == 0

    @pl.when((pl.program_id(0) == 0) & is_ctx)
    def _():
        above_ref[...] = jnp.zeros_like(above_ref)

    m = mod_ref[0, 0]
    hx = jnp.concatenate([h_ref[0], dn_ref[0]], axis=0)
    xn = (_rms(hx, g_ref[...]) * (1.0 + m[4:5]) + m[3:4]).astype(BF16)
    xc = xn[:TOK_TILE]
    c0, c1, c2 = N_RW, N_RW + D_MODEL, N_RW + 2 * D_MODEL
    plain = [(plx_ref, slice(c0, c1)), (plg_ref, slice(c1, c2)), (pm_ref, slice(c2, N_IN))]

    row = lax.broadcasted_iota(jnp.int32, (TOK_TILE, 1), 0)
    period = jnp.where(is_ctx, TOK_TILE, GRID_W)
    pos = jnp.where(row >= period, row % GRID_W, row)
    first = pos == 0
    last = pos == period - 1
    no_up = ((row < GRID_W) & (t <= 1)) | is_ctx
    no_dn = ((row >= TOK_TILE - GRID_W) & (t == n_tiles - 1)) | is_ctx
    nb_scale = jnp.where(is_ctx, 0.5, 0.25)
    for j in range(len(plain)):
        cols = slice(SHIFT_COL_STARTS[j], SHIFT_COL_STARTS[j + 1])
        pe = jnp.dot(xn, w_ref[:, cols], preferred_element_type=F32)
        p = pe[:TOK_TILE]
        left = jnp.where(first, 0.0, pltpu.roll(p, 1, 0))
        right = jnp.where(last, 0.0, pltpu.roll(p, TOK_TILE - 1, 0))
        up = jnp.where(no_up, 0.0, jnp.concatenate([above_ref[:, cols], p[:TOK_TILE - GRID_W]], axis=0))
        dn = jnp.where(no_dn, 0.0, pe[GRID_W:])
        above_ref[:, cols] = p[TOK_TILE - GRID_W:]
        nb = nb_scale * (up + dn + left + right)
        z_ref[0, :, cols] = p + (nb - p) * mu_ref[:, cols]
        out_ref, wcols = plain[j]
        out_ref[0] = jnp.dot(xc, w_ref[:, wcols], preferred_element_type=F32).astype(out_ref.dtype)
    e, et = _head_sum_mats()
    kx = z_ref[0, :, D_MODEL:2 * D_MODEL] * kk_w_ref[...]
    nrm = jnp.maximum(jnp.sqrt(_head_reduce(kx * kx, e)), 1e-12)
    kk_ref[0] = kx / _head_expand(nrm, et)


def _inproj(h1, mod, g, w_in, mu, k_k):
    bsz, tok, _ = h1.shape
    nt = tok // TOK_TILE
    rows_per_tile = TOK_TILE // GRID_W
    n_rows = tok // GRID_W
    widths = (N_RW, D_MODEL, D_MODEL, D_MODEL, 2 * D_MODEL)
    dtypes = (F32, F32, F32, BF16, BF16)
    return pl.pallas_call(
        functools.partial(_inproj_kernel, nt),
        grid=(bsz, nt),
        in_specs=[pl.BlockSpec((1, TOK_TILE, D_MODEL), lambda b, t: (b, t, 0)),
                  pl.BlockSpec((1, GRID_W, D_MODEL),
                               lambda b, t: (b, jnp.minimum((t + 1) * rows_per_tile, n_rows - 1), 0)),
                  pl.BlockSpec((1, 1, N_MOD, D_MODEL), lambda b, t: (b, jnp.minimum(t, 1), 0, 0)),
                  _resident((1, D_MODEL)),
                  _resident((D_MODEL, N_IN)),
                  _resident((1, N_RW)),
                  _resident((1, D_MODEL))],
        out_specs=[pl.BlockSpec((1, TOK_TILE, w), lambda b, t: (b, t, 0)) for w in widths],
        out_shape=[jax.ShapeDtypeStruct((bsz, tok, w), dt) for w, dt in zip(widths, dtypes)],
        scratch_shapes=[pltpu.VMEM((GRID_W, N_RW), F32)],
        compiler_params=_params("arbitrary", "arbitrary"),
        name="inproj",
    )(h1, h1, mod, g, w_in, mu, k_k)


def _scan_order(reverse, n_ctx, n_all, c):
    if not reverse:
        return c
    return jnp.where(c < n_ctx, n_ctx - 1 - c, n_all + n_ctx - 1 - c)


def _rwkv_kernel(reverse, z_ref, kk_ref, w0_ref, wup_ref, a0_ref, aup_ref, ka_ref, y_ref, st_ref):
    c = pl.program_id(1)

    @pl.when(c == 0)
    def _():
        st_ref[...] = jnp.zeros_like(st_ref)

    n_rows = z_ref.shape[0]
    n_sub = z_ref.shape[1] // CHUNK
    n_pairs = N_HEADS // 2
    ri = lax.broadcasted_iota(jnp.int32, (CHUNK, CHUNK), 0)
    ci = lax.broadcasted_iota(jnp.int32, (CHUNK, CHUNK), 1)
    tri = ((ci >= ri) if reverse else (ci <= ri)).astype(BF16)

    def prepare(i, rows, cols):
        col = lambda base: z_ref[i, rows, base + cols.start:base + cols.stop]
        lora = z_ref[i, rows, COL_LORA:COL_LORA + 256]
        lw = -DECAY_SCALE * _sigmoid(
            w0_ref[0, :, cols] + _dot(jnp.tanh(lora[:, :128]), wup_ref[0, :, cols]))
        a = _sigmoid(a0_ref[0, :, cols] + _dot(lora[:, 128:], aup_ref[0, :, cols]))
        lw_hi, lw_lo = _split2(lw)
        cum = jnp.dot(jnp.concatenate([tri, tri], axis=1), jnp.concatenate([lw_hi, lw_lo], axis=0),
                      preferred_element_type=F32)
        cum_last = cum[0:1] if reverse else cum[CHUNK - 1:CHUNK]
        r, k, v = col(0), col(D_MODEL), col(2 * D_MODEL)
        kk = kk_ref[i, rows, cols]
        e_neg = jnp.exp(-cum)
        full = dict(at=kk * jnp.exp(cum - lw), bt=(kk * a) * e_neg,
                    kt=(k * (1.0 + (a - 1.0) * ka_ref[:, cols])) * e_neg,
                    rt=r * jnp.exp(cum), vv=v, gg=jnp.exp(cum_last))
        width = cols.stop - cols.start
        return [{name: x[:, s:s + PAIR] for name, x in full.items()} for s in range(0, width, PAIR)]

    tt = lax.broadcasted_iota(jnp.int32, (CHUNK, PAIR), 0)
    lane = lax.broadcasted_iota(jnp.int32, (CHUNK, PAIR), 1)
    ss = lane % HEAD_DIM
    before = (ss > tt) if reverse else (ss < tt)
    before_eq = (ss >= tt) if reverse else (ss <= tt)
    eye = jnp.where(ss == tt, 1.0, 0.0)
    head0 = lane < HEAD_DIM

    def bd(x):
        xb = x.astype(BF16)
        zero = jnp.zeros_like(xb)
        return jnp.concatenate([jnp.where(head0, xb, zero), jnp.where(head0, zero, xb)], axis=0)

    def tr(x):
        t = bd(x).T
        return jnp.where(head0, t[:CHUNK], t[CHUNK:])

    def solve(q, out):
        n = range(len(q))
        o = [_dot_nt(jnp.concatenate([x["at"], x["rt"]], axis=0),
                     jnp.concatenate([bd(x["bt"]), bd(x["kt"])], axis=0)) for x in q]
        a_ab = [jnp.where(before, x[:CHUNK, :PAIR], 0.0) for x in o]
        a_ak = [jnp.where(before, x[:CHUNK, PAIR:], 0.0) for x in o]
        rbk = [jnp.concatenate([jnp.where(before_eq, x[CHUNK:, :PAIR], 0.0),
                                jnp.where(before_eq, x[CHUNK:, PAIR:], 0.0)], axis=1).astype(BF16) for x in o]
        yield
        inv = [eye - x for x in a_ab]
        pw = [_dot(x, bd(x)) for x in a_ab]
        v_bd = [bd(x["vv"]) for x in q]
        akv = [_dot(a_ak[p], v_bd[p]) for p in n]
        yield
        for _ in range(4):
            zz = [_dot(pw[p], jnp.concatenate([bd(pw[p]), bd(inv[p])], axis=1)) for p in n]
            pw = [x[:, :PAIR] for x in zz]
            inv = [inv[p] + zz[p][:, PAIR:] for p in n]
            yield
        inv = [inv[p] + _dot(pw[p], bd(inv[p])) for p in n]
        yield
        wu = [-_dot(inv[p], jnp.concatenate([bd(q[p]["at"]), bd(akv[p])], axis=1)) for p in n]
        yield
        for p in n:
            g = q[p]["gg"]
            lhs = jnp.concatenate([
                rbk[p],
                jnp.concatenate([tr(q[p]["bt"] * g), tr(q[p]["kt"] * g)], axis=1)], axis=0)
            rhs = jnp.concatenate([
                jnp.concatenate([bd(wu[p][:, :PAIR]), bd(wu[p][:, PAIR:])], axis=1),
                jnp.concatenate([jnp.zeros((PAIR, PAIR), BF16), v_bd[p]], axis=1)], axis=0)
            o = _dot(lhs, rhs)
            out.append((jnp.concatenate([o[:CHUNK, :PAIR] + q[p]["rt"], o[CHUNK:, :PAIR] + eye * g], axis=0),
                        o[:CHUNK, PAIR:], o[CHUNK:, PAIR:]))

    subs = list(range(n_sub))[::-1] if reverse else list(range(n_sub))
    rows_of = {j: slice(j * CHUNK, (j + 1) * CHUNK) for j in subs}
    pieces = [(i, slice(c0, c0 + PREP_COLS)) for i in range(n_rows) for c0 in range(0, D_MODEL, PREP_COLS)]

    def carry(j, loc):
        o2 = [_dot(lhs2, bd(st_ref[p])) for p, (lhs2, _, _) in enumerate(loc)]
        for p, (_, y_loc, n_c) in enumerate(loc):
            st_ref[p] = o2[p][CHUNK:] + n_c
            i, s = p // n_pairs, slice((p % n_pairs) * PAIR, (p % n_pairs + 1) * PAIR)
            y_ref[i, rows_of[j], s] = (o2[p][:CHUNK] + y_loc).astype(y_ref.dtype)

    chains = [prepare(i, rows_of[subs[0]], cols) for i, cols in pieces]
    pending = None
    for idx, j in enumerate(subs):
        q = [x for part in chains for x in part]
        loc, chains = [], []
        todo = list(pieces) if idx + 1 < n_sub else []
        for _ in solve(q, loc):
            if pending is not None:
                carry(*pending)
                pending = None
            if todo:
                i, cols = todo.pop(0)
                chains.append(prepare(i, rows_of[subs[idx + 1]], cols))
        chains += [prepare(i, rows_of[subs[idx + 1]], cols) for i, cols in todo]
        pending = (j, loc)
    carry(*pending)


def _rwkv_scan(reverse, z, kk, w0, w_up_ext, a0, a_up_ext, k_a, n_ctx_chunks):
    bsz, tok, _ = z.shape
    rows = RWKV_ROWS if bsz % RWKV_ROWS == 0 else 1
    assert n_ctx_chunks % RWKV_SUB == 0 and (tok // CHUNK) % RWKV_SUB == 0
    blk = RWKV_SUB * CHUNK
    n_all = tok // blk
    order = functools.partial(_scan_order, reverse, n_ctx_chunks // RWKV_SUB, n_all)
    return pl.pallas_call(
        functools.partial(_rwkv_kernel, reverse),
        grid=(bsz // rows, n_all),
        in_specs=[pl.BlockSpec((rows, blk, N_RW), lambda b, c: (b, order(c), 0)),
                  pl.BlockSpec((rows, blk, D_MODEL), lambda b, c: (b, order(c), 0)),
                  pl.BlockSpec((1, 1, D_MODEL), lambda b, c: (0, 0, 0)),
                  pl.BlockSpec((1, 128, D_MODEL), lambda b, c: (0, 0, 0)),
                  pl.BlockSpec((1, 1, D_MODEL), lambda b, c: (0, 0, 0)),
                  pl.BlockSpec((1, 128, D_MODEL), lambda b, c: (0, 0, 0)),
                  pl.BlockSpec((1, D_MODEL), lambda b, c: (0, 0))],
        out_specs=pl.BlockSpec((rows, blk, D_MODEL), lambda b, c: (b, order(c), 0)),
        out_shape=jax.ShapeDtypeStruct((bsz, tok, D_MODEL), BF16),
        scratch_shapes=[pltpu.VMEM((rows * N_HEADS // 2, CHUNK, PAIR), F32)],
        compiler_params=_params("parallel", "arbitrary"),
        name="rwkv_bwd" if reverse else "rwkv_fwd",
    )(z, kk, w0, w_up_ext, a0, a_up_ext, k_a)


def _lru_kernel(reverse, n_tiles, x_ref, prev_ref, next_ref, cw_ref, cb_ref, lam_ref,
                wa_ref, ba_ref, wx_ref, bx_ref, h_ref, carry_ref, a_scr, u_scr):
    t = pl.program_id(1)

    @pl.when(t == 0)
    def _():
        carry_ref[...] = jnp.zeros_like(carry_ref)

    ti = _scan_order(reverse, 1, n_tiles, t)
    prev = jnp.where(ti >= 2, prev_ref[0], 0.0)
    nxt = jnp.where((ti >= 1) & (ti <= n_tiles - 2), next_ref[0], 0.0)
    x = x_ref[0]
    ext = jnp.concatenate([prev, x, nxt], axis=0)
    n_ext = TOK_TILE + 2 * SUBLANES
    body = slice(SUBLANES, SUBLANES + TOK_TILE)
    cw = cw_ref[...]
    xc = (cw[0:1] * pltpu.roll(ext, 1, 0)[body] + cw[1:2] * x
          + cw[2:3] * pltpu.roll(ext, n_ext - 1, 0)[body]
          + cw[3:4] * pltpu.roll(ext, n_ext - 2, 0)[body] + cb_ref[...])
    xb = xc.astype(BF16)
    gr = jnp.concatenate([jnp.dot(xb[:, n * LRU_BLOCK:(n + 1) * LRU_BLOCK], wa_ref[0, n],
                                  preferred_element_type=F32) for n in range(LRU_BLOCKS)], axis=1)
    gi = jnp.concatenate([jnp.dot(xb[:, n * LRU_BLOCK:(n + 1) * LRU_BLOCK], wx_ref[0, n],
                                  preferred_element_type=F32) for n in range(LRU_BLOCKS)], axis=1)
    gate_r = _sigmoid(gr + ba_ref[0])
    gate_i = _sigmoid(gi + bx_ref[0])
    lam = lam_ref[0]
    log_sig = jnp.minimum(lam, 0.0) - jnp.log1p(jnp.exp(-jnp.abs(lam)))
    log_a = LRU_C * gate_r * log_sig
    a = jnp.exp(log_a)
    a_scr[...] = a
    u_scr[...] = jnp.sqrt(-jnp.tanh(log_a) * (a * a + 1.0)) * (gate_i * xc)

    row = lax.broadcasted_iota(jnp.int32, (SUBLANES, D_MODEL), 0)
    n_groups = TOK_TILE // SUBLANES

    def group(i, h_in):
        gidx = (n_groups - 1 - i) if reverse else i
        rows = pl.ds(pl.multiple_of(gidx * SUBLANES, SUBLANES), SUBLANES)
        a = a_scr[rows, :]
        u = u_scr[rows, :]
        for s in (1, 2, 4):
            if reverse:
                ok = row < SUBLANES - s
                sh = SUBLANES - s
            else:
                ok = row >= s
                sh = s
            u = jnp.where(ok, a * pltpu.roll(u, sh, 0) + u, u)
            a = jnp.where(ok, a * pltpu.roll(a, sh, 0), a)
        h = u + a * h_in
        h_ref[0, rows, :] = h
        last = h[0:1] if reverse else h[SUBLANES - 1:SUBLANES]
        return jnp.broadcast_to(last, (SUBLANES, D_MODEL))

    carry_ref[...] = lax.fori_loop(0, n_groups, group, carry_ref[...])


def _lru_scan(reverse, d, p_lx, conv_w, conv_b, lam, wa, ba, wx, bx):
    bsz, tok, _ = p_lx.shape
    nt = tok // TOK_TILE
    gpt = TOK_TILE // SUBLANES
    n_groups = tok // SUBLANES
    order = functools.partial(_scan_order, reverse, 1, nt)
    dsel = lambda *_: (d, 0, 0)
    return pl.pallas_call(
        functools.partial(_lru_kernel, reverse, nt),
        grid=(bsz, nt),
        in_specs=[pl.BlockSpec((1, TOK_TILE, D_MODEL), lambda b, t: (b, order(t), 0)),
                  pl.BlockSpec((1, SUBLANES, D_MODEL),
                               lambda b, t: (b, jnp.maximum(order(t) * gpt - 1, 0), 0)),
                  pl.BlockSpec((1, SUBLANES, D_MODEL),
                               lambda b, t: (b, jnp.minimum((order(t) + 1) * gpt, n_groups - 1), 0)),
                  pl.BlockSpec((4, D_MODEL), lambda b, t: (0, 0)),
                  pl.BlockSpec((1, D_MODEL), lambda b, t: (0, 0)),
                  pl.BlockSpec((1, 1, D_MODEL), dsel),
                  pl.BlockSpec((1, LRU_BLOCKS, LRU_BLOCK, LRU_BLOCK), lambda b, t: (d, 0, 0, 0)),
                  pl.BlockSpec((1, 1, D_MODEL), dsel),
                  pl.BlockSpec((1, LRU_BLOCKS, LRU_BLOCK, LRU_BLOCK), lambda b, t: (d, 0, 0, 0)),
                  pl.BlockSpec((1, 1, D_MODEL), dsel)],
        out_specs=pl.BlockSpec((1, TOK_TILE, D_MODEL), lambda b, t: (b, order(t), 0)),
        out_shape=jax.ShapeDtypeStruct((bsz, tok, D_MODEL), F32),
        scratch_shapes=[pltpu.VMEM((SUBLANES, D_MODEL), F32),
                        pltpu.VMEM((TOK_TILE, D_MODEL), F32),
                        pltpu.VMEM((TOK_TILE, D_MODEL), F32)],
        compiler_params=_params("parallel", "arbitrary"),
        name="lru_bwd" if reverse else "lru_fwd",
    )(p_lx, p_lx, p_lx, conv_w, conv_b, lam, wa, ba, wx, bx)


def _mixout_kernel(yf_ref, yb_ref, z_ref, hf_ref, hb_ref, plg_ref, pm_ref, h1_ref, mod_ref,
                   rk_ref, lnw_ref, lnb_ref, gup_ref, wprw_ref, wplru_ref, wout_ref, o_ref):
    e, et = _head_sum_mats()
    inv_n = 1.0 / HEAD_DIM
    y = yf_ref[0].astype(F32) + yb_ref[0].astype(F32)
    mu = _head_expand(_head_reduce(y, e) * inv_n, et)
    yc = y - mu
    var = _head_expand(_head_reduce(yc * yc, e) * inv_n, et)
    yn = yc * lax.rsqrt(var + LN_X_EPS) * lnw_ref[...] + lnb_ref[...]
    z = z_ref[0]
    r = z[:, 0:D_MODEL]
    k = z[:, D_MODEL:2 * D_MODEL]
    v = z[:, 2 * D_MODEL:3 * D_MODEL]
    bonus = _head_expand(_head_reduce(r * k * rk_ref[...], e), et) * v
    gate = _dot(_sigmoid(z[:, COL_G:COL_G + 128]), gup_ref[...])
    y_rw = (yn + bonus) * gate
    lg = plg_ref[0].astype(F32)
    gelu = lg * (0.5 * (1.0 + jnp.tanh(math.sqrt(2.0 / math.pi) * (lg + 0.044715 * (lg * lg * lg)))))
    y_lru = (hf_ref[0] + hb_ref[0]) * gelu
    pm = pm_ref[0].astype(F32)
    merged = (_sigmoid(pm[:, :D_MODEL]) * _dot(y_rw, wprw_ref[...])
              + _sigmoid(pm[:, D_MODEL:]) * _dot(y_lru, wplru_ref[...]))
    m = mod_ref[0, 0]
    o_ref[0] = h1_ref[0] + m[5:6] * _dot(merged, wout_ref[...])


def _mixout(y_f, y_b, z, h_f, h_b, p_lg, p_m, h1, mod, r_k, ln_w, ln_b, g_up, w_proj_rw, w_proj_lru, w_out):
    bsz, tok, _ = h1.shape
    nt = tok // TOK_TILE - 1
    lat = lambda w: pl.BlockSpec((1, TOK_TILE, w), lambda b, t: (b, t + 1, 0))
    row = lambda: pl.BlockSpec((1, D_MODEL), lambda b, t: (0, 0))
    return pl.pallas_call(
        _mixout_kernel,
        grid=(bsz, nt),
        in_specs=[lat(D_MODEL), lat(D_MODEL), lat(N_RW), lat(D_MODEL), lat(D_MODEL), lat(D_MODEL),
                  lat(2 * D_MODEL), lat(D_MODEL),
                  pl.BlockSpec((1, 1, N_MOD, D_MODEL), lambda b, t: (b, 1, 0, 0)),
                  row(), row(), row(),
                  _resident((128, D_MODEL)), _resident((D_MODEL, D_MODEL)),
                  _resident((D_MODEL, D_MODEL)), _resident((D_MODEL, D_MODEL))],
        out_specs=pl.BlockSpec((1, TOK_TILE, D_MODEL), lambda b, t: (b, t, 0)),
        out_shape=jax.ShapeDtypeStruct((bsz, nt * TOK_TILE, D_MODEL), F32),
        compiler_params=_params("parallel", "parallel"),
        name="mixout",
    )(y_f, y_b, z, h_f, h_b, p_lg, p_m, h1, mod, r_k, ln_w, ln_b, g_up, w_proj_rw, w_proj_lru, w_out)


def _lora_ext(w_up, d):
    zero = jnp.zeros_like(w_up[0])
    parts = [w_up[0], zero] if d == 0 else [zero, w_up[1]]
    return jnp.concatenate(parts, axis=0)[None].astype(BF16)


def kernel(x, c, ctx, c_ctx, w_mod, b_mod, g_ffn1, ffn1_wg, ffn1_wu, ffn1_wd, g_mix, w_in, rw_mu, rw_w0, rw_w_up, rw_a0, rw_a_up, rw_g_up, rw_k_k, rw_k_a, rw_r_k, rw_ln_w, rw_ln_b, w_proj_rw, lru_conv_w, lru_conv_b, lru_lam, lru_wa, lru_ba, lru_wx, lru_bx, w_proj_lru, w_out, g_ffn2, ffn2_wg, ffn2_wu, ffn2_wd, g_final):
    bsz, seq, d_model = x.shape
    assert d_model == D_MODEL and seq % TOK_TILE == 0 and ctx.shape[1] == TOK_TILE
    assert w_mod.shape[0] == 1 and bsz < SUBLANES
    bf = lambda w: w.astype(BF16)
    row = lambda p: p.reshape(1, -1)

    cc = jnp.zeros((SUBLANES, D_MODEL), F32).at[:bsz].set(c).at[bsz].set(c_ctx)
    mod = _modulation(cc, w_mod[0], row(b_mod[0])).reshape(SUBLANES, N_MOD, D_MODEL)
    mod = jnp.stack([jnp.broadcast_to(mod[bsz], (bsz, N_MOD, D_MODEL)), mod[:bsz]], axis=1)

    h1 = _ffn1(x, ctx, mod, row(g_ffn1[0]), bf(ffn1_wg[0]), bf(ffn1_wu[0]), bf(ffn1_wd[0]))
    z, kk, p_lx, p_lg, p_m = _inproj(h1, mod, row(g_mix[0]), bf(w_in[0]), row(rw_mu[0]), row(rw_k_k[0]))

    n_ctx_chunks = TOK_TILE // CHUNK
    ys, hs = [], []
    for d, reverse in ((0, False), (1, True)):
        ys.append(_rwkv_scan(reverse, z, kk, rw_w0[0, d].reshape(1, 1, -1), _lora_ext(rw_w_up[0], d),
                             rw_a0[0, d].reshape(1, 1, -1), _lora_ext(rw_a_up[0], d), row(rw_k_a[0]),
                             n_ctx_chunks))
        hs.append(_lru_scan(reverse, d, p_lx, lru_conv_w[0], row(lru_conv_b[0]),
                            lru_lam[0][:, None, :], bf(lru_wa[0]), lru_ba[0][:, None, :],
                            bf(lru_wx[0]), lru_bx[0][:, None, :]))

    x2 = _mixout(ys[0], ys[1], z, hs[0], hs[1], p_lg, p_m, h1, mod, row(rw_r_k[0]), row(rw_ln_w[0]),
                 row(rw_ln_b[0]), bf(rw_g_up[0]), bf(w_proj_rw[0]), bf(w_proj_lru[0]), bf(w_out[0]))
    return _ffn2(x2, mod, row(g_ffn2[0]), bf(ffn2_wg[0]), bf(ffn2_wu[0]), bf(ffn2_wd[0]), row(g_final))
```

```python
import functools
import math

import jax
import jax.numpy as jnp
from jax import lax
from jax.experimental import pallas as pl
from jax.experimental.pallas import tpu as pltpu

F32 = jnp.float32
BF16 = jnp.bfloat16

D_MODEL = 1024
D_FF = 2816
N_MOD = 9
NORM_EPS = 1e-6
HEAD_DIM = 64
N_HEADS = D_MODEL // HEAD_DIM
LN_X_EPS = 64e-5
DECAY_SCALE = math.exp(-0.5)
LRU_BLOCKS = 4
LRU_BLOCK = D_MODEL // LRU_BLOCKS
LRU_C = 8.0
GRID_W = 64
N_RW = 3 * D_MODEL + 4 * 64 + 128
COL_LORA = 3 * D_MODEL
COL_G = COL_LORA + 256
N_IN = N_RW + 4 * D_MODEL

TOK_TILE = 256
FFN2_TILE = 512
CHUNK = 64
PAIR = 2 * HEAD_DIM
RWKV_ROWS = 2
RWKV_SUB = 4
PREP_COLS = 1024
MXU_COLS = 256
SHIFT_COL_STARTS = (0, 5 * MXU_COLS, 10 * MXU_COLS, N_RW)
SUBLANES = 8
VMEM_LIMIT = 56 * 1024 * 1024


def _params(*sem):
    return pltpu.CompilerParams(dimension_semantics=sem, vmem_limit_bytes=VMEM_LIMIT)


def _resident(shape):
    nd = len(shape)
    return pl.BlockSpec(shape, lambda *_: (0,) * nd, pipeline_mode=pl.Buffered(1))


def _dot(a, b):
    return jnp.dot(a.astype(BF16), b.astype(BF16), preferred_element_type=F32)


def _dot_nt(a, b):
    return lax.dot_general(a.astype(BF16), b.astype(BF16), (((1,), (1,)), ((), ())),
                           preferred_element_type=F32)


def _split2(x):
    hi = x.astype(BF16)
    lo = (x - hi.astype(F32)).astype(BF16)
    return hi, lo


def _split3(x):
    hi = x.astype(BF16)
    r1 = x - hi.astype(F32)
    mid = r1.astype(BF16)
    lo = (r1 - mid.astype(F32)).astype(BF16)
    return hi, mid, lo


def _rms(x, g):
    return x * lax.rsqrt(jnp.mean(x * x, axis=-1, keepdims=True) + NORM_EPS) * g


def _sigmoid(x):
    return 0.5 * jnp.tanh(0.5 * x) + 0.5


def _head_sum_mats():
    c = lax.broadcasted_iota(jnp.int32, (D_MODEL, 128), 0) // HEAD_DIM
    j = lax.broadcasted_iota(jnp.int32, (D_MODEL, 128), 1)
    e = ((c == j % N_HEADS) & (j < 3 * N_HEADS)).astype(BF16)
    jt = lax.broadcasted_iota(jnp.int32, (128, D_MODEL), 0)
    ct = lax.broadcasted_iota(jnp.int32, (128, D_MODEL), 1) // HEAD_DIM
    et = ((ct == jt % N_HEADS) & (jt < 3 * N_HEADS)).astype(BF16)
    return e, et


def _head_reduce(x, e):
    hi, lo = _split2(x)
    return (jnp.dot(hi, e, preferred_element_type=F32)
            + jnp.dot(lo, e, preferred_element_type=F32))


def _head_expand(s, et):
    hi, mid, lo = _split3(s)
    lane = lax.broadcasted_iota(jnp.int32, s.shape, 1)
    parts = jnp.where(lane < N_HEADS, hi, jnp.where(lane < 2 * N_HEADS, mid, lo))
    return jnp.dot(parts, et, preferred_element_type=F32)


def _mod_kernel(c_ref, w_ref, b_ref, o_ref):
    c = c_ref[...]
    s = c * _sigmoid(c)
    o_ref[...] = _dot(s, w_ref[...]) + b_ref[...]


def _modulation(cc, w_mod, b_mod):
    n = w_mod.shape[1]
    tn = 1152
    return pl.pallas_call(
        _mod_kernel,
        grid=(n // tn,),
        in_specs=[pl.BlockSpec((SUBLANES, D_MODEL), lambda j: (0, 0)),
                  pl.BlockSpec((D_MODEL, tn), lambda j: (0, j)),
                  pl.BlockSpec((1, tn), lambda j: (0, j))],
        out_specs=pl.BlockSpec((SUBLANES, tn), lambda j: (0, j)),
        out_shape=jax.ShapeDtypeStruct((SUBLANES, n), F32),
        compiler_params=_params("parallel"),
        name="modulation",
    )(cc, w_mod, b_mod)


def _ffn_math(h, m, row0, g, wg_ref, wu_ref, wd_ref):
    shift, scale, gate = m[row0:row0 + 1], m[row0 + 1:row0 + 2], m[row0 + 2:row0 + 3]
    hn = (_rms(h, g) * (1.0 + scale) + shift).astype(BF16)
    a = jnp.dot(hn, wg_ref[...], preferred_element_type=F32)
    u = jnp.dot(hn, wu_ref[...], preferred_element_type=F32)
    act = (a * _sigmoid(a) * u).astype(BF16)
    y = jnp.dot(act, wd_ref[...], preferred_element_type=F32)
    return h + 0.5 * gate * y


def _ffn1_kernel(x_ref, ctx_ref, mod_ref, g_ref, wg_ref, wu_ref, wd_ref, o_ref):
    is_ctx = pl.program_id(1) == 0
    h = jnp.where(is_ctx, ctx_ref[0], x_ref[0])
    o_ref[0] = _ffn_math(h, mod_ref[0, 0], 0, g_ref[...], wg_ref, wu_ref, wd_ref)


def _ffn1(x, ctx, mod, g, wg, wu, wd):
    bsz, seq, _ = x.shape
    nt = seq // TOK_TILE + 1
    tile = (1, TOK_TILE, D_MODEL)
    return pl.pallas_call(
        _ffn1_kernel,
        grid=(bsz, nt),
        in_specs=[pl.BlockSpec(tile, lambda b, t: (b, jnp.maximum(t - 1, 0), 0)),
                  pl.BlockSpec(tile, lambda b, t: (b, 0, 0)),
                  pl.BlockSpec((1, 1, N_MOD, D_MODEL), lambda b, t: (b, jnp.minimum(t, 1), 0, 0)),
                  _resident((1, D_MODEL)),
                  _resident((D_MODEL, D_FF)), _resident((D_MODEL, D_FF)), _resident((D_FF, D_MODEL))],
        out_specs=pl.BlockSpec(tile, lambda b, t: (b, t, 0)),
        out_shape=jax.ShapeDtypeStruct((bsz, nt * TOK_TILE, D_MODEL), F32),
        compiler_params=_params("parallel", "parallel"),
        name="ffn1",
    )(x, ctx, mod, g, wg, wu, wd)


def _ffn2_kernel(x_ref, mod_ref, g_ref, wg_ref, wu_ref, wd_ref, gf_ref, o_ref):
    y = _ffn_math(x_ref[0], mod_ref[0, 0], 6, g_ref[...], wg_ref, wu_ref, wd_ref)
    o_ref[0] = _rms(y, gf_ref[...])


def _ffn2(x, mod, g, wg, wu, wd, g_final):
    bsz, seq, _ = x.shape
    tile = (1, FFN2_TILE, D_MODEL)
    return pl.pallas_call(
        _ffn2_kernel,
        grid=(bsz, seq // FFN2_TILE),
        in_specs=[pl.BlockSpec(tile, lambda b, t: (b, t, 0)),
                  pl.BlockSpec((1, 1, N_MOD, D_MODEL), lambda b, t: (b, 1, 0, 0)),
                  _resident((1, D_MODEL)),
                  _resident((D_MODEL, D_FF)), _resident((D_MODEL, D_FF)), _resident((D_FF, D_MODEL)),
                  _resident((1, D_MODEL))],
        out_specs=pl.BlockSpec(tile, lambda b, t: (b, t, 0)),
        out_shape=jax.ShapeDtypeStruct((bsz, seq, D_MODEL), F32),
        compiler_params=_params("parallel", "parallel"),
        name="ffn2",
    )(x, mod, g, wg, wu, wd, g_final)


def _inproj_kernel(n_tiles, h_ref, dn_ref, mod_ref, g_ref, w_ref, mu_ref, kk_w_ref,
                   z_ref, kk_ref, plx_ref, plg_ref, pm_ref, above_ref, k_ref):
    t = pl.program_id(1)
    is_ctx = t == 0

    @pl.when((pl.program_id(0) == 0) & is_ctx)
    def _():
        above_ref[...] = jnp.zeros_like(above_ref)

    m = mod_ref[0, 0]
    hx = jnp.concatenate([h_ref[0], dn_ref[0]], axis=0)
    xn = (_rms(hx, g_ref[...]) * (1.0 + m[4:5]) + m[3:4]).astype(BF16)
    xc = xn[:TOK_TILE]
    c0, c1, c2 = N_RW, N_RW + D_MODEL, N_RW + 2 * D_MODEL
    plain = [(plx_ref, slice(c0, c1)), (plg_ref, slice(c1, c2)), (pm_ref, slice(c2, N_IN))]

    row = lax.broadcasted_iota(jnp.int32, (TOK_TILE, 1), 0)
    period = jnp.where(is_ctx, TOK_TILE, GRID_W)
    pos = jnp.where(row >= period, row % GRID_W, row)
    first = pos == 0
    last = pos == period - 1
    no_up = ((row < GRID_W) & (t <= 1)) | is_ctx
    no_dn = ((row >= TOK_TILE - GRID_W) & (t == n_tiles - 1)) | is_ctx
    nb_scale = jnp.where(is_ctx, 0.5, 0.25)
    for j in range(len(plain)):
        cols = slice(SHIFT_COL_STARTS[j], SHIFT_COL_STARTS[j + 1])
        pe = jnp.dot(xn, w_ref[:, cols], preferred_element_type=F32)
        p = pe[:TOK_TILE]
        left = jnp.where(first, 0.0, pltpu.roll(p, 1, 0))
        right = jnp.where(last, 0.0, pltpu.roll(p, TOK_TILE - 1, 0))
        up = jnp.where(no_up, 0.0, jnp.concatenate([above_ref[:, cols], p[:TOK_TILE - GRID_W]], axis=0))
        dn = jnp.where(no_dn, 0.0, pe[GRID_W:])
        above_ref[:, cols] = p[TOK_TILE - GRID_W:]
        nb = nb_scale * (up + dn + left + right)
        zf = p + (nb - p) * mu_ref[:, cols]
        z_ref[0, :, cols] = zf.astype(z_ref.dtype)
        lo, hi = max(cols.start, D_MODEL), min(cols.stop, 2 * D_MODEL)
        if lo < hi:
            k_ref[:, lo - D_MODEL:hi - D_MODEL] = zf[:, lo - cols.start:hi - cols.start]
        out_ref, wcols = plain[j]
        out_ref[0] = jnp.dot(xc, w_ref[:, wcols], preferred_element_type=F32).astype(out_ref.dtype)
    e, et = _head_sum_mats()
    kx = k_ref[...] * kk_w_ref[...]
    nrm = jnp.maximum(jnp.sqrt(_head_reduce(kx * kx, e)), 1e-12)
    kk_ref[0] = kx / _head_expand(nrm, et)


def _inproj(h1, mod, g, w_in, mu, k_k):
    bsz, tok, _ = h1.shape
    nt = tok // TOK_TILE
    rows_per_tile = TOK_TILE // GRID_W
    n_rows = tok // GRID_W
    widths = (N_RW, D_MODEL, D_MODEL, D_MODEL, 2 * D_MODEL)
    dtypes = (BF16, F32, F32, BF16, BF16)
    return pl.pallas_call(
        functools.partial(_inproj_kernel, nt),
        grid=(bsz, nt),
        in_specs=[pl.BlockSpec((1, TOK_TILE, D_MODEL), lambda b, t: (b, t, 0)),
                  pl.BlockSpec((1, GRID_W, D_MODEL),
                               lambda b, t: (b, jnp.minimum((t + 1) * rows_per_tile, n_rows - 1), 0)),
                  pl.BlockSpec((1, 1, N_MOD, D_MODEL), lambda b, t: (b, jnp.minimum(t, 1), 0, 0)),
                  _resident((1, D_MODEL)),
                  _resident((D_MODEL, N_IN)),
                  _resident((1, N_RW)),
                  _resident((1, D_MODEL))],
        out_specs=[pl.BlockSpec((1, TOK_TILE, w), lambda b, t: (b, t, 0)) for w in widths],
        out_shape=[jax.ShapeDtypeStruct((bsz, tok, w), dt) for w, dt in zip(widths, dtypes)],
        scratch_shapes=[pltpu.VMEM((GRID_W, N_RW), F32), pltpu.VMEM((TOK_TILE, D_MODEL), F32)],
        compiler_params=_params("arbitrary", "arbitrary"),
        name="inproj",
    )(h1, h1, mod, g, w_in, mu, k_k)


def _scan_order(reverse, n_ctx, n_all, c):
    if not reverse:
        return c
    return jnp.where(c < n_ctx, n_ctx - 1 - c, n_all + n_ctx - 1 - c)


def _rwkv_kernel(reverse, z_ref, kk_ref, w0_ref, wup_ref, a0_ref, aup_ref, ka_ref, y_ref, st_ref):
    c = pl.program_id(1)

    @pl.when(c == 0)
    def _():
        st_ref[...] = jnp.zeros_like(st_ref)

    n_rows = z_ref.shape[0]
    n_sub = z_ref.shape[1] // CHUNK
    n_pairs = N_HEADS // 2
    ri = lax.broadcasted_iota(jnp.int32, (CHUNK, CHUNK), 0)
    ci = lax.broadcasted_iota(jnp.int32, (CHUNK, CHUNK), 1)
    tri = ((ci >= ri) if reverse else (ci <= ri)).astype(BF16)

    def prepare(i, rows, cols):
        col = lambda base: z_ref[i, rows, base + cols.start:base + cols.stop].astype(F32)
        lora = z_ref[i, rows, COL_LORA:COL_LORA + 256].astype(F32)
        lw = -DECAY_SCALE * _sigmoid(
            w0_ref[0, :, cols] + _dot(jnp.tanh(lora[:, :128]), wup_ref[0, :, cols]))
        a = _sigmoid(a0_ref[0, :, cols] + _dot(lora[:, 128:], aup_ref[0, :, cols]))
        lw_hi, lw_lo = _split2(lw)
        cum = jnp.dot(jnp.concatenate([tri, tri], axis=1), jnp.concatenate([lw_hi, lw_lo], axis=0),
                      preferred_element_type=F32)
        cum_last = cum[0:1] if reverse else cum[CHUNK - 1:CHUNK]
        r, k, v = col(0), col(D_MODEL), col(2 * D_MODEL)
        kk = kk_ref[i, rows, cols]
        e_neg = jnp.exp(-cum)
        full = dict(at=kk * jnp.exp(cum - lw), bt=(kk * a) * e_neg,
                    kt=(k * (1.0 + (a - 1.0) * ka_ref[:, cols])) * e_neg,
                    rt=r * jnp.exp(cum), vv=v, gg=jnp.exp(cum_last))
        width = cols.stop - cols.start
        return [{name: x[:, s:s + PAIR] for name, x in full.items()} for s in range(0, width, PAIR)]

    tt = lax.broadcasted_iota(jnp.int32, (CHUNK, PAIR), 0)
    lane = lax.broadcasted_iota(jnp.int32, (CHUNK, PAIR), 1)
    ss = lane % HEAD_DIM
    before = (ss > tt) if reverse else (ss < tt)
    before_eq = (ss >= tt) if reverse else (ss <= tt)
    eye = jnp.where(ss == tt, 1.0, 0.0)
    head0 = lane < HEAD_DIM

    def bd(x):
        xb = x.astype(BF16)
        zero = jnp.zeros_like(xb)
        return jnp.concatenate([jnp.where(head0, xb, zero), jnp.where(head0, zero, xb)], axis=0)

    def tr(x):
        t = bd(x).T
        return jnp.where(head0, t[:CHUNK], t[CHUNK:])

    def solve(q, out):
        n = range(len(q))
        o = [_dot_nt(jnp.concatenate([x["at"], x["rt"]], axis=0),
                     jnp.concatenate([bd(x["bt"]), bd(x["kt"])], axis=0)) for x in q]
        a_ab = [jnp.where(before, x[:CHUNK, :PAIR], 0.0) for x in o]
        a_ak = [jnp.where(before, x[:CHUNK, PAIR:], 0.0) for x in o]
        rbk = [jnp.concatenate([jnp.where(before_eq, x[CHUNK:, :PAIR], 0.0),
                                jnp.where(before_eq, x[CHUNK:, PAIR:], 0.0)], axis=1).astype(BF16) for x in o]
        yield
        inv = [eye - x for x in a_ab]
        pw = [_dot(x, bd(x)) for x in a_ab]
        v_bd = [bd(x["vv"]) for x in q]
        akv = [_dot(a_ak[p], v_bd[p]) for p in n]
        yield
        for _ in range(4):
            zz = [_dot(pw[p], jnp.concatenate([bd(pw[p]), bd(inv[p])], axis=1)) for p in n]
            pw = [x[:, :PAIR] for x in zz]
            inv = [inv[p] + zz[p][:, PAIR:] for p in n]
            yield
        inv = [inv[p] + _dot(pw[p], bd(inv[p])) for p in n]
        yield
        wu = [-_dot(inv[p], jnp.concatenate([bd(q[p]["at"]), bd(akv[p])], axis=1)) for p in n]
        yield
        for p in n:
            g = q[p]["gg"]
            lhs = jnp.concatenate([
                rbk[p],
                jnp.concatenate([tr(q[p]["bt"] * g), tr(q[p]["kt"] * g)], axis=1)], axis=0)
            rhs = jnp.concatenate([
                jnp.concatenate([bd(wu[p][:, :PAIR]), bd(wu[p][:, PAIR:])], axis=1),
                jnp.concatenate([jnp.zeros((PAIR, PAIR), BF16), v_bd[p]], axis=1)], axis=0)
            o = _dot(lhs, rhs)
            out.append((jnp.concatenate([o[:CHUNK, :PAIR] + q[p]["rt"], o[CHUNK:, :PAIR] + eye * g], axis=0),
                        o[:CHUNK, PAIR:], o[CHUNK:, PAIR:]))

    subs = list(range(n_sub))[::-1] if reverse else list(range(n_sub))
    rows_of = {j: slice(j * CHUNK, (j + 1) * CHUNK) for j in subs}
    pieces = [(i, slice(c0, c0 + PREP_COLS)) for i in range(n_rows) for c0 in range(0, D_MODEL, PREP_COLS)]

    def carry(j, loc):
        o2 = [_dot(lhs2, bd(st_ref[p])) for p, (lhs2, _, _) in enumerate(loc)]
        for p, (_, y_loc, n_c) in enumerate(loc):
            st_ref[p] = o2[p][CHUNK:] + n_c
            i, s = p // n_pairs, slice((p % n_pairs) * PAIR, (p % n_pairs + 1) * PAIR)
            y_ref[i, rows_of[j], s] = (o2[p][:CHUNK] + y_loc).astype(y_ref.dtype)

    chains = [prepare(i, rows_of[subs[0]], cols) for i, cols in pieces]
    pending = None
    for idx, j in enumerate(subs):
        q = [x for part in chains for x in part]
        loc, chains = [], []
        todo = list(pieces) if idx + 1 < n_sub else []
        for _ in solve(q, loc):
            if pending is not None:
                carry(*pending)
                pending = None
            if todo:
                i, cols = todo.pop(0)
                chains.append(prepare(i, rows_of[subs[idx + 1]], cols))
        chains += [prepare(i, rows_of[subs[idx + 1]], cols) for i, cols in todo]
        pending = (j, loc)
    carry(*pending)


def _rwkv_scan(reverse, z, kk, w0, w_up_ext, a0, a_up_ext, k_a, n_ctx_chunks):
    bsz, tok, _ = z.shape
    rows = RWKV_ROWS if bsz % RWKV_ROWS == 0 else 1
    assert n_ctx_chunks % RWKV_SUB == 0 and (tok // CHUNK) % RWKV_SUB == 0
    blk = RWKV_SUB * CHUNK
    n_all = tok // blk
    order = functools.partial(_scan_order, reverse, n_ctx_chunks // RWKV_SUB, n_all)
    return pl.pallas_call(
        functools.partial(_rwkv_kernel, reverse),
        grid=(bsz // rows, n_all),
        in_specs=[pl.BlockSpec((rows, blk, N_RW), lambda b, c: (b, order(c), 0)),
                  pl.BlockSpec((rows, blk, D_MODEL), lambda b, c: (b, order(c), 0)),
                  pl.BlockSpec((1, 1, D_MODEL), lambda b, c: (0, 0, 0)),
                  pl.BlockSpec((1, 128, D_MODEL), lambda b, c: (0, 0, 0)),
                  pl.BlockSpec((1, 1, D_MODEL), lambda b, c: (0, 0, 0)),
                  pl.BlockSpec((1, 128, D_MODEL), lambda b, c: (0, 0, 0)),
                  pl.BlockSpec((1, D_MODEL), lambda b, c: (0, 0))],
        out_specs=pl.BlockSpec((rows, blk, D_MODEL), lambda b, c: (b, order(c), 0)),
        out_shape=jax.ShapeDtypeStruct((bsz, tok, D_MODEL), BF16),
        scratch_shapes=[pltpu.VMEM((rows * N_HEADS // 2, CHUNK, PAIR), F32)],
        compiler_params=_params("parallel", "arbitrary"),
        name="rwkv_bwd" if reverse else "rwkv_fwd",
    )(z, kk, w0, w_up_ext, a0, a_up_ext, k_a)


def _lru_kernel(reverse, n_tiles, x_ref, prev_ref, next_ref, cw_ref, cb_ref, lam_ref,
                wa_ref, ba_ref, wx_ref, bx_ref, h_ref, carry_ref, a_scr, u_scr):
    t = pl.program_id(1)

    @pl.when(t == 0)
    def _():
        carry_ref[...] = jnp.zeros_like(carry_ref)

    ti = _scan_order(reverse, 1, n_tiles, t)
    prev = jnp.where(ti >= 2, prev_ref[0], 0.0)
    nxt = jnp.where((ti >= 1) & (ti <= n_tiles - 2), next_ref[0], 0.0)
    x = x_ref[0]
    ext = jnp.concatenate([prev, x, nxt], axis=0)
    n_ext = TOK_TILE + 2 * SUBLANES
    body = slice(SUBLANES, SUBLANES + TOK_TILE)
    cw = cw_ref[...]
    xc = (cw[0:1] * pltpu.roll(ext, 1, 0)[body] + cw[1:2] * x
          + cw[2:3] * pltpu.roll(ext, n_ext - 1, 0)[body]
          + cw[3:4] * pltpu.roll(ext, n_ext - 2, 0)[body] + cb_ref[...])
    xb = xc.astype(BF16)
    gr = jnp.concatenate([jnp.dot(xb[:, n * LRU_BLOCK:(n + 1) * LRU_BLOCK], wa_ref[0, n],
                                  preferred_element_type=F32) for n in range(LRU_BLOCKS)], axis=1)
    gi = jnp.concatenate([jnp.dot(xb[:, n * LRU_BLOCK:(n + 1) * LRU_BLOCK], wx_ref[0, n],
                                  preferred_element_type=F32) for n in range(LRU_BLOCKS)], axis=1)
    gate_r = _sigmoid(gr + ba_ref[0])
    gate_i = _sigmoid(gi + bx_ref[0])
    lam = lam_ref[0]
    log_sig = jnp.minimum(lam, 0.0) - jnp.log1p(jnp.exp(-jnp.abs(lam)))
    log_a = LRU_C * gate_r * log_sig
    a = jnp.exp(log_a)
    a_scr[...] = a
    u_scr[...] = jnp.sqrt(-jnp.tanh(log_a) * (a * a + 1.0)) * (gate_i * xc)

    row = lax.broadcasted_iota(jnp.int32, (SUBLANES, D_MODEL), 0)
    n_groups = TOK_TILE // SUBLANES

    def group(i, h_in):
        gidx = (n_groups - 1 - i) if reverse else i
        rows = pl.ds(pl.multiple_of(gidx * SUBLANES, SUBLANES), SUBLANES)
        a = a_scr[rows, :]
        u = u_scr[rows, :]
        for s in (1, 2, 4):
            if reverse:
                ok = row < SUBLANES - s
                sh = SUBLANES - s
            else:
                ok = row >= s
                sh = s
            u = jnp.where(ok, a * pltpu.roll(u, sh, 0) + u, u)
            a = jnp.where(ok, a * pltpu.roll(a, sh, 0), a)
        h = u + a * h_in
        h_ref[0, rows, :] = h
        last = h[0:1] if reverse else h[SUBLANES - 1:SUBLANES]
        return jnp.broadcast_to(last, (SUBLANES, D_MODEL))

    carry_ref[...] = lax.fori_loop(0, n_groups, group, carry_ref[...])


def _lru_scan(reverse, d, p_lx, conv_w, conv_b, lam, wa, ba, wx, bx):
    bsz, tok, _ = p_lx.shape
    nt = tok // TOK_TILE
    gpt = TOK_TILE // SUBLANES
    n_groups = tok // SUBLANES
    order = functools.partial(_scan_order, reverse, 1, nt)
    dsel = lambda *_: (d, 0, 0)
    return pl.pallas_call(
        functools.partial(_lru_kernel, reverse, nt),
        grid=(bsz, nt),
        in_specs=[pl.BlockSpec((1, TOK_TILE, D_MODEL), lambda b, t: (b, order(t), 0)),
                  pl.BlockSpec((1, SUBLANES, D_MODEL),
                               lambda b, t: (b, jnp.maximum(order(t) * gpt - 1, 0), 0)),
                  pl.BlockSpec((1, SUBLANES, D_MODEL),
                               lambda b, t: (b, jnp.minimum((order(t) + 1) * gpt, n_groups - 1), 0)),
                  pl.BlockSpec((4, D_MODEL), lambda b, t: (0, 0)),
                  pl.BlockSpec((1, D_MODEL), lambda b, t: (0, 0)),
                  pl.BlockSpec((1, 1, D_MODEL), dsel),
                  pl.BlockSpec((1, LRU_BLOCKS, LRU_BLOCK, LRU_BLOCK), lambda b, t: (d, 0, 0, 0)),
                  pl.BlockSpec((1, 1, D_MODEL), dsel),
                  pl.BlockSpec((1, LRU_BLOCKS, LRU_BLOCK, LRU_BLOCK), lambda b, t: (d, 0, 0, 0)),
                  pl.BlockSpec((1, 1, D_MODEL), dsel)],
        out_specs=pl.BlockSpec((1, TOK_TILE, D_MODEL), lambda b, t: (b, order(t), 0)),
        out_shape=jax.ShapeDtypeStruct((bsz, tok, D_MODEL), F32),
        scratch_shapes=[pltpu.VMEM((SUBLANES, D_MODEL), F32),
                        pltpu.VMEM((TOK_TILE, D_MODEL), F32),
                        pltpu.VMEM((TOK_TILE, D_MODEL), F32)],
        compiler_params=_params("parallel", "arbitrary"),
        name="lru_bwd" if reverse else "lru_fwd",
    )(p_lx, p_lx, p_lx, conv_w, conv_b, lam, wa, ba, wx, bx)


def _mixout_kernel(yf_ref, yb_ref, z_ref, hf_ref, hb_ref, plg_ref, pm_ref, h1_ref, mod_ref,
                   rk_ref, lnw_ref, lnb_ref, gup_ref, wprw_ref, wplru_ref, wout_ref, o_ref):
    e, et = _head_sum_mats()
    inv_n = 1.0 / HEAD_DIM
    y = yf_ref[0].astype(F32) + yb_ref[0].astype(F32)
    mu = _head_expand(_head_reduce(y, e) * inv_n, et)
    yc = y - mu
    var = _head_expand(_head_reduce(yc * yc, e) * inv_n, et)
    yn = yc * lax.rsqrt(var + LN_X_EPS) * lnw_ref[...] + lnb_ref[...]
    r = z_ref[0, :, 0:D_MODEL].astype(F32)
    k = z_ref[0, :, D_MODEL:2 * D_MODEL].astype(F32)
    v = z_ref[0, :, 2 * D_MODEL:3 * D_MODEL].astype(F32)
    bonus = _head_expand(_head_reduce(r * k * rk_ref[...], e), et) * v
    gate = _dot(_sigmoid(z_ref[0, :, COL_G:COL_G + 128].astype(F32)), gup_ref[...])
    y_rw = (yn + bonus) * gate
    lg = plg_ref[0].astype(F32)
    gelu = lg * (0.5 * (1.0 + jnp.tanh(math.sqrt(2.0 / math.pi) * (lg + 0.044715 * (lg * lg * lg)))))
    y_lru = (hf_ref[0] + hb_ref[0]) * gelu
    pm = pm_ref[0].astype(F32)
    merged = (_sigmoid(pm[:, :D_MODEL]) * _dot(y_rw, wprw_ref[...])
              + _sigmoid(pm[:, D_MODEL:]) * _dot(y_lru, wplru_ref[...]))
    m = mod_ref[0, 0]
    o_ref[0] = h1_ref[0] + m[5:6] * _dot(merged, wout_ref[...])


def _mixout(y_f, y_b, z, h_f, h_b, p_lg, p_m, h1, mod, r_k, ln_w, ln_b, g_up, w_proj_rw, w_proj_lru, w_out):
    bsz, tok, _ = h1.shape
    nt = tok // TOK_TILE - 1
    lat = lambda w: pl.BlockSpec((1, TOK_TILE, w), lambda b, t: (b, t + 1, 0))
    row = lambda: pl.BlockSpec((1, D_MODEL), lambda b, t: (0, 0))
    return pl.pallas_call(
        _mixout_kernel,
        grid=(bsz, nt),
        in_specs=[lat(D_MODEL), lat(D_MODEL), lat(N_RW), lat(D_MODEL), lat(D_MODEL), lat(D_MODEL),
                  lat(2 * D_MODEL), lat(D_MODEL),
                  pl.BlockSpec((1, 1, N_MOD, D_MODEL), lambda b, t: (b, 1, 0, 0)),
                  row(), row(), row(),
                  _resident((128, D_MODEL)), _resident((D_MODEL, D_MODEL)),
                  _resident((D_MODEL, D_MODEL)), _resident((D_MODEL, D_MODEL))],
        out_specs=pl.BlockSpec((1, TOK_TILE, D_MODEL), lambda b, t: (b, t, 0)),
        out_shape=jax.ShapeDtypeStruct((bsz, nt * TOK_TILE, D_MODEL), F32),
        compiler_params=_params("parallel", "parallel"),
        name="mixout",
    )(y_f, y_b, z, h_f, h_b, p_lg, p_m, h1, mod, r_k, ln_w, ln_b, g_up, w_proj_rw, w_proj_lru, w_out)


def _lora_ext(w_up, d):
    zero = jnp.zeros_like(w_up[0])
    parts = [w_up[0], zero] if d == 0 else [zero, w_up[1]]
    return jnp.concatenate(parts, axis=0)[None].astype(BF16)


def kernel(x, c, ctx, c_ctx, w_mod, b_mod, g_ffn1, ffn1_wg, ffn1_wu, ffn1_wd, g_mix, w_in, rw_mu, rw_w0, rw_w_up, rw_a0, rw_a_up, rw_g_up, rw_k_k, rw_k_a, rw_r_k, rw_ln_w, rw_ln_b, w_proj_rw, lru_conv_w, lru_conv_b, lru_lam, lru_wa, lru_ba, lru_wx, lru_bx, w_proj_lru, w_out, g_ffn2, ffn2_wg, ffn2_wu, ffn2_wd, g_final):
    bsz, seq, d_model = x.shape
    assert d_model == D_MODEL and seq % FFN2_TILE == 0 and ctx.shape[1] == TOK_TILE
    assert w_mod.shape[0] == 1 and bsz < SUBLANES
    bf = lambda w: w.astype(BF16)
    row = lambda p: p.reshape(1, -1)

    cc = jnp.zeros((SUBLANES, D_MODEL), F32).at[:bsz].set(c).at[bsz].set(c_ctx)
    mod = _modulation(cc, w_mod[0], row(b_mod[0])).reshape(SUBLANES, N_MOD, D_MODEL)
    mod = jnp.stack([jnp.broadcast_to(mod[bsz], (bsz, N_MOD, D_MODEL)), mod[:bsz]], axis=1)

    h1 = _ffn1(x, ctx, mod, row(g_ffn1[0]), bf(ffn1_wg[0]), bf(ffn1_wu[0]), bf(ffn1_wd[0]))
    z, kk, p_lx, p_lg, p_m = _inproj(h1, mod, row(g_mix[0]), bf(w_in[0]), row(rw_mu[0]), row(rw_k_k[0]))

    n_ctx_chunks = TOK_TILE // CHUNK
    ys, hs = [], []
    for d, reverse in ((0, False), (1, True)):
        ys.append(_rwkv_scan(reverse, z, kk, rw_w0[0, d].reshape(1, 1, -1), _lora_ext(rw_w_up[0], d),
                             rw_a0[0, d].reshape(1, 1, -1), _lora_ext(rw_a_up[0], d), row(rw_k_a[0]),
                             n_ctx_chunks))
        hs.append(_lru_scan(reverse, d, p_lx, lru_conv_w[0], row(lru_conv_b[0]),
                            lru_lam[0][:, None, :], bf(lru_wa[0]), lru_ba[0][:, None, :],
                            bf(lru_wx[0]), lru_bx[0][:, None, :]))

    x2 = _mixout(ys[0], ys[1], z, hs[0], hs[1], p_lg, p_m, h1, mod, row(rw_r_k[0]), row(rw_ln_w[0]),
                 row(rw_ln_b[0]), bf(rw_g_up[0]), bf(w_proj_rw[0]), bf(w_proj_lru[0]), bf(w_out[0]))
    return _ffn2(x2, mod, row(g_ffn2[0]), bf(ffn2_wg[0]), bf(ffn2_wu[0]), bf(ffn2_wd[0]), row(g_final))
```

```python
import functools
import math

import jax
import jax.numpy as jnp
from jax import lax
from jax.experimental import pallas as pl
from jax.experimental.pallas import tpu as pltpu

F32 = jnp.float32
BF16 = jnp.bfloat16

D_MODEL = 1024
D_FF = 2816
N_MOD = 9
NORM_EPS = 1e-6
HEAD_DIM = 64
N_HEADS = D_MODEL // HEAD_DIM
LN_X_EPS = 64e-5
DECAY_SCALE = math.exp(-0.5)
LRU_BLOCKS = 4
LRU_BLOCK = D_MODEL // LRU_BLOCKS
LRU_C = 8.0
GRID_W = 64
N_RW = 3 * D_MODEL + 4 * 64 + 128
COL_LORA = 3 * D_MODEL
COL_G = COL_LORA + 256
N_IN = N_RW + 4 * D_MODEL

TOK_TILE = 256
CHUNK = 64
PAIR = 2 * HEAD_DIM
RWKV_ROWS = 2
RWKV_SUB = 4
PREP_COLS = 1024
MXU_COLS = 256
SHIFT_COL_STARTS = (0, 5 * MXU_COLS, 10 * MXU_COLS, N_RW)
SUBLANES = 8
VMEM_LIMIT = 56 * 1024 * 1024


def _params(*sem):
    return pltpu.CompilerParams(dimension_semantics=sem, vmem_limit_bytes=VMEM_LIMIT)


def _resident(shape):
    nd = len(shape)
    return pl.BlockSpec(shape, lambda *_: (0,) * nd, pipeline_mode=pl.Buffered(1))


def _dot(a, b):
    return jnp.dot(a.astype(BF16), b.astype(BF16), preferred_element_type=F32)


def _dot_nt(a, b):
    return lax.dot_general(a.astype(BF16), b.astype(BF16), (((1,), (1,)), ((), ())),
                           preferred_element_type=F32)


def _split2(x):
    hi = x.astype(BF16)
    lo = (x - hi.astype(F32)).astype(BF16)
    return hi, lo


def _split3(x):
    hi = x.astype(BF16)
    r1 = x - hi.astype(F32)
    mid = r1.astype(BF16)
    lo = (r1 - mid.astype(F32)).astype(BF16)
    return hi, mid, lo


def _rms(x, g):
    return x * lax.rsqrt(jnp.mean(x * x, axis=-1, keepdims=True) + NORM_EPS) * g


def _sigmoid(x):
    return 0.5 * jnp.tanh(0.5 * x) + 0.5


def _head_sum_mats():
    c = lax.broadcasted_iota(jnp.int32, (D_MODEL, 128), 0) // HEAD_DIM
    j = lax.broadcasted_iota(jnp.int32, (D_MODEL, 128), 1)
    e = ((c == j % N_HEADS) & (j < 3 * N_HEADS)).astype(BF16)
    jt = lax.broadcasted_iota(jnp.int32, (128, D_MODEL), 0)
    ct = lax.broadcasted_iota(jnp.int32, (128, D_MODEL), 1) // HEAD_DIM
    et = ((ct == jt % N_HEADS) & (jt < 3 * N_HEADS)).astype(BF16)
    return e, et


def _head_reduce(x, e):
    hi, lo = _split2(x)
    return (jnp.dot(hi, e, preferred_element_type=F32)
            + jnp.dot(lo, e, preferred_element_type=F32))


def _head_expand(s, et):
    hi, mid, lo = _split3(s)
    lane = lax.broadcasted_iota(jnp.int32, s.shape, 1)
    parts = jnp.where(lane < N_HEADS, hi, jnp.where(lane < 2 * N_HEADS, mid, lo))
    return jnp.dot(parts, et, preferred_element_type=F32)


def _mod_kernel(c_ref, w_ref, b_ref, o_ref):
    c = c_ref[...]
    s = c * _sigmoid(c)
    o_ref[...] = _dot(s, w_ref[...]) + b_ref[...]


def _modulation(cc, w_mod, b_mod):
    n = w_mod.shape[1]
    tn = 1152
    return pl.pallas_call(
        _mod_kernel,
        grid=(n // tn,),
        in_specs=[pl.BlockSpec((SUBLANES, D_MODEL), lambda j: (0, 0)),
                  pl.BlockSpec((D_MODEL, tn), lambda j: (0, j)),
                  pl.BlockSpec((1, tn), lambda j: (0, j))],
        out_specs=pl.BlockSpec((SUBLANES, tn), lambda j: (0, j)),
        out_shape=jax.ShapeDtypeStruct((SUBLANES, n), F32),
        compiler_params=_params("parallel"),
        name="modulation",
    )(cc, w_mod, b_mod)


def _ffn_math(h, m, row0, g, wg_ref, wu_ref, wd_ref):
    shift, scale, gate = m[row0:row0 + 1], m[row0 + 1:row0 + 2], m[row0 + 2:row0 + 3]
    hn = (_rms(h, g) * (1.0 + scale) + shift).astype(BF16)
    a = jnp.dot(hn, wg_ref[...], preferred_element_type=F32)
    u = jnp.dot(hn, wu_ref[...], preferred_element_type=F32)
    act = (a * _sigmoid(a) * u).astype(BF16)
    y = jnp.dot(act, wd_ref[...], preferred_element_type=F32)
    return h + 0.5 * gate * y


def _ffn1_kernel(x_ref, ctx_ref, mod_ref, g_ref, wg_ref, wu_ref, wd_ref, o_ref):
    is_ctx = pl.program_id(1) == 0
    h = jnp.where(is_ctx, ctx_ref[0], x_ref[0])
    o_ref[0] = _ffn_math(h, mod_ref[0, 0], 0, g_ref[...], wg_ref, wu_ref, wd_ref)


def _ffn1(x, ctx, mod, g, wg, wu, wd):
    bsz, seq, _ = x.shape
    nt = seq // TOK_TILE + 1
    tile = (1, TOK_TILE, D_MODEL)
    return pl.pallas_call(
        _ffn1_kernel,
        grid=(bsz, nt),
        in_specs=[pl.BlockSpec(tile, lambda b, t: (b, jnp.maximum(t - 1, 0), 0)),
                  pl.BlockSpec(tile, lambda b, t: (b, 0, 0)),
                  pl.BlockSpec((1, 1, N_MOD, D_MODEL), lambda b, t: (b, jnp.minimum(t, 1), 0, 0)),
                  _resident((1, D_MODEL)),
                  _resident((D_MODEL, D_FF)), _resident((D_MODEL, D_FF)), _resident((D_FF, D_MODEL))],
        out_specs=pl.BlockSpec(tile, lambda b, t: (b, t, 0)),
        out_shape=jax.ShapeDtypeStruct((bsz, nt * TOK_TILE, D_MODEL), F32),
        compiler_params=_params("parallel", "parallel"),
        name="ffn1",
    )(x, ctx, mod, g, wg, wu, wd)


def _inproj_kernel(n_tiles, h_ref, dn_ref, mod_ref, g_ref, w_ref, mu_ref, kk_w_ref,
                   z_ref, kk_ref, plx_ref, plg_ref, pm_ref, above_ref, k_ref):
    t = pl.program_id(1)
    is_ctx = t == 0

    @pl.when((pl.program_id(0) == 0) & is_ctx)
    def _():
        above_ref[...] = jnp.zeros_like(above_ref)

    m = mod_ref[0, 0]
    hx = jnp.concatenate([h_ref[0], dn_ref[0]], axis=0)
    xn = (_rms(hx, g_ref[...]) * (1.0 + m[4:5]) + m[3:4]).astype(BF16)
    xc = xn[:TOK_TILE]
    c0, c1, c2 = N_RW, N_RW + D_MODEL, N_RW + 2 * D_MODEL
    plain = [(plx_ref, slice(c0, c1)), (plg_ref, slice(c1, c2)), (pm_ref, slice(c2, N_IN))]

    row = lax.broadcasted_iota(jnp.int32, (TOK_TILE, 1), 0)
    period = jnp.where(is_ctx, TOK_TILE, GRID_W)
    pos = jnp.where(row >= period, row % GRID_W, row)
    first = pos == 0
    last = pos == period - 1
    no_up = ((row < GRID_W) & (t <= 1)) | is_ctx
    no_dn = ((row >= TOK_TILE - GRID_W) & (t == n_tiles - 1)) | is_ctx
    nb_scale = jnp.where(is_ctx, 0.5, 0.25)
    for j in range(len(plain)):
        cols = slice(SHIFT_COL_STARTS[j], SHIFT_COL_STARTS[j + 1])
        pe = jnp.dot(xn, w_ref[:, cols], preferred_element_type=F32)
        p = pe[:TOK_TILE]
        left = jnp.where(first, 0.0, pltpu.roll(p, 1, 0))
        right = jnp.where(last, 0.0, pltpu.roll(p, TOK_TILE - 1, 0))
        up = jnp.where(no_up, 0.0, jnp.concatenate([above_ref[:, cols], p[:TOK_TILE - GRID_W]], axis=0))
        dn = jnp.where(no_dn, 0.0, pe[GRID_W:])
        above_ref[:, cols] = p[TOK_TILE - GRID_W:]
        nb = nb_scale * (up + dn + left + right)
        zf = p + (nb - p) * mu_ref[:, cols]
        z_ref[0, :, cols] = zf.astype(z_ref.dtype)
        lo, hi = max(cols.start, D_MODEL), min(cols.stop, 2 * D_MODEL)
        if lo < hi:
            k_ref[:, lo - D_MODEL:hi - D_MODEL] = zf[:, lo - cols.start:hi - cols.start]
        out_ref, wcols = plain[j]
        out_ref[0] = jnp.dot(xc, w_ref[:, wcols], preferred_element_type=F32).astype(out_ref.dtype)
    e, et = _head_sum_mats()
    kx = k_ref[...] * kk_w_ref[...]
    nrm = jnp.maximum(jnp.sqrt(_head_reduce(kx * kx, e)), 1e-12)
    kk_ref[0] = kx / _head_expand(nrm, et)


def _inproj(h1, mod, g, w_in, mu, k_k):
    bsz, tok, _ = h1.shape
    nt = tok // TOK_TILE
    rows_per_tile = TOK_TILE // GRID_W
    n_rows = tok // GRID_W
    widths = (N_RW, D_MODEL, D_MODEL, D_MODEL, 2 * D_MODEL)
    dtypes = (BF16, F32, F32, BF16, BF16)
    return pl.pallas_call(
        functools.partial(_inproj_kernel, nt),
        grid=(bsz, nt),
        in_specs=[pl.BlockSpec((1, TOK_TILE, D_MODEL), lambda b, t: (b, t, 0)),
                  pl.BlockSpec((1, GRID_W, D_MODEL),
                               lambda b, t: (b, jnp.minimum((t + 1) * rows_per_tile, n_rows - 1), 0)),
                  pl.BlockSpec((1, 1, N_MOD, D_MODEL), lambda b, t: (b, jnp.minimum(t, 1), 0, 0)),
                  _resident((1, D_MODEL)),
                  _resident((D_MODEL, N_IN)),
                  _resident((1, N_RW)),
                  _resident((1, D_MODEL))],
        out_specs=[pl.BlockSpec((1, TOK_TILE, w), lambda b, t: (b, t, 0)) for w in widths],
        out_shape=[jax.ShapeDtypeStruct((bsz, tok, w), dt) for w, dt in zip(widths, dtypes)],
        scratch_shapes=[pltpu.VMEM((GRID_W, N_RW), F32), pltpu.VMEM((TOK_TILE, D_MODEL), F32)],
        compiler_params=_params("arbitrary", "arbitrary"),
        name="inproj",
    )(h1, h1, mod, g, w_in, mu, k_k)


def _scan_order(reverse, n_ctx, n_all, c):
    if not reverse:
        return c
    return jnp.where(c < n_ctx, n_ctx - 1 - c, n_all + n_ctx - 1 - c)


def _rwkv_kernel(reverse, z_ref, kk_ref, w0_ref, wup_ref, a0_ref, aup_ref, ka_ref, y_ref, st_ref):
    c = pl.program_id(1)

    @pl.when(c == 0)
    def _():
        st_ref[...] = jnp.zeros_like(st_ref)

    n_rows = z_ref.shape[0]
    n_sub = z_ref.shape[1] // CHUNK
    n_pairs = N_HEADS // 2
    ri = lax.broadcasted_iota(jnp.int32, (CHUNK, CHUNK), 0)
    ci = lax.broadcasted_iota(jnp.int32, (CHUNK, CHUNK), 1)
    tri = ((ci >= ri) if reverse else (ci <= ri)).astype(BF16)

    def prepare(i, rows, cols):
        col = lambda base: z_ref[i, rows, base + cols.start:base + cols.stop].astype(F32)
        lora = z_ref[i, rows, COL_LORA:COL_LORA + 256].astype(F32)
        lw = -DECAY_SCALE * _sigmoid(
            w0_ref[0, :, cols] + _dot(jnp.tanh(lora[:, :128]), wup_ref[0, :, cols]))
        a = _sigmoid(a0_ref[0, :, cols] + _dot(lora[:, 128:], aup_ref[0, :, cols]))
        lw_hi, lw_lo = _split2(lw)
        cum = jnp.dot(jnp.concatenate([tri, tri], axis=1), jnp.concatenate([lw_hi, lw_lo], axis=0),
                      preferred_element_type=F32)
        cum_last = cum[0:1] if reverse else cum[CHUNK - 1:CHUNK]
        r, k, v = col(0), col(D_MODEL), col(2 * D_MODEL)
        kk = kk_ref[i, rows, cols]
        e_neg = jnp.exp(-cum)
        full = dict(at=kk * jnp.exp(cum - lw), bt=(kk * a) * e_neg,
                    kt=(k * (1.0 + (a - 1.0) * ka_ref[:, cols])) * e_neg,
                    rt=r * jnp.exp(cum), vv=v, gg=jnp.exp(cum_last))
        width = cols.stop - cols.start
        return [{name: x[:, s:s + PAIR] for name, x in full.items()} for s in range(0, width, PAIR)]

    tt = lax.broadcasted_iota(jnp.int32, (CHUNK, PAIR), 0)
    lane = lax.broadcasted_iota(jnp.int32, (CHUNK, PAIR), 1)
    ss = lane % HEAD_DIM
    before = (ss > tt) if reverse else (ss < tt)
    before_eq = (ss >= tt) if reverse else (ss <= tt)
    eye = jnp.where(ss == tt, 1.0, 0.0)
    head0 = lane < HEAD_DIM

    def bd(x):
        xb = x.astype(BF16)
        zero = jnp.zeros_like(xb)
        return jnp.concatenate([jnp.where(head0, xb, zero), jnp.where(head0, zero, xb)], axis=0)

    def tr(x):
        t = bd(x).T
        return jnp.where(head0, t[:CHUNK], t[CHUNK:])

    def solve(q, out):
        n = range(len(q))
        o = [_dot_nt(jnp.concatenate([x["at"], x["rt"]], axis=0),
                     jnp.concatenate([bd(x["bt"]), bd(x["kt"])], axis=0)) for x in q]
        a_ab = [jnp.where(before, x[:CHUNK, :PAIR], 0.0) for x in o]
        a_ak = [jnp.where(before, x[:CHUNK, PAIR:], 0.0) for x in o]
        rbk = [jnp.concatenate([jnp.where(before_eq, x[CHUNK:, :PAIR], 0.0),
                                jnp.where(before_eq, x[CHUNK:, PAIR:], 0.0)], axis=1).astype(BF16) for x in o]
        yield
        inv = [eye - x for x in a_ab]
        pw = [_dot(x, bd(x)) for x in a_ab]
        v_bd = [bd(x["vv"]) for x in q]
        akv = [_dot(a_ak[p], v_bd[p]) for p in n]
        yield
        for _ in range(4):
            zz = [_dot(pw[p], jnp.concatenate([bd(pw[p]), bd(inv[p])], axis=1)) for p in n]
            pw = [x[:, :PAIR] for x in zz]
            inv = [inv[p] + zz[p][:, PAIR:] for p in n]
            yield
        inv = [inv[p] + _dot(pw[p], bd(inv[p])) for p in n]
        yield
        wu = [-_dot(inv[p], jnp.concatenate([bd(q[p]["at"]), bd(akv[p])], axis=1)) for p in n]
        yield
        for p in n:
            g = q[p]["gg"]
            lhs = jnp.concatenate([
                rbk[p],
                jnp.concatenate([tr(q[p]["bt"] * g), tr(q[p]["kt"] * g)], axis=1)], axis=0)
            rhs = jnp.concatenate([
                jnp.concatenate([bd(wu[p][:, :PAIR]), bd(wu[p][:, PAIR:])], axis=1),
                jnp.concatenate([jnp.zeros((PAIR, PAIR), BF16), v_bd[p]], axis=1)], axis=0)
            o = _dot(lhs, rhs)
            out.append((jnp.concatenate([o[:CHUNK, :PAIR] + q[p]["rt"], o[CHUNK:, :PAIR] + eye * g], axis=0),
                        o[:CHUNK, PAIR:], o[CHUNK:, PAIR:]))

    subs = list(range(n_sub))[::-1] if reverse else list(range(n_sub))
    rows_of = {j: slice(j * CHUNK, (j + 1) * CHUNK) for j in subs}
    pieces = [(i, slice(c0, c0 + PREP_COLS)) for i in range(n_rows) for c0 in range(0, D_MODEL, PREP_COLS)]

    def carry(j, loc):
        o2 = [_dot(lhs2, bd(st_ref[p])) for p, (lhs2, _, _) in enumerate(loc)]
        for p, (_, y_loc, n_c) in enumerate(loc):
            st_ref[p] = o2[p][CHUNK:] + n_c
            i, s = p // n_pairs, slice((p % n_pairs) * PAIR, (p % n_pairs + 1) * PAIR)
            y_ref[i, rows_of[j], s] = (o2[p][:CHUNK] + y_loc).astype(y_ref.dtype)

    chains = [prepare(i, rows_of[subs[0]], cols) for i, cols in pieces]
    pending = None
    for idx, j in enumerate(subs):
        q = [x for part in chains for x in part]
        loc, chains = [], []
        todo = list(pieces) if idx + 1 < n_sub else []
        for _ in solve(q, loc):
            if pending is not None:
                carry(*pending)
                pending = None
            if todo:
                i, cols = todo.pop(0)
                chains.append(prepare(i, rows_of[subs[idx + 1]], cols))
        chains += [prepare(i, rows_of[subs[idx + 1]], cols) for i, cols in todo]
        pending = (j, loc)
    carry(*pending)


def _rwkv_scan(reverse, z, kk, w0, w_up_ext, a0, a_up_ext, k_a, n_ctx_chunks):
    bsz, tok, _ = z.shape
    rows = RWKV_ROWS if bsz % RWKV_ROWS == 0 else 1
    assert n_ctx_chunks % RWKV_SUB == 0 and (tok // CHUNK) % RWKV_SUB == 0
    blk = RWKV_SUB * CHUNK
    n_all = tok // blk
    order = functools.partial(_scan_order, reverse, n_ctx_chunks // RWKV_SUB, n_all)
    return pl.pallas_call(
        functools.partial(_rwkv_kernel, reverse),
        grid=(bsz // rows, n_all),
        in_specs=[pl.BlockSpec((rows, blk, N_RW), lambda b, c: (b, order(c), 0)),
                  pl.BlockSpec((rows, blk, D_MODEL), lambda b, c: (b, order(c), 0)),
                  pl.BlockSpec((1, 1, D_MODEL), lambda b, c: (0, 0, 0)),
                  pl.BlockSpec((1, 128, D_MODEL), lambda b, c: (0, 0, 0)),
                  pl.BlockSpec((1, 1, D_MODEL), lambda b, c: (0, 0, 0)),
                  pl.BlockSpec((1, 128, D_MODEL), lambda b, c: (0, 0, 0)),
                  pl.BlockSpec((1, D_MODEL), lambda b, c: (0, 0))],
        out_specs=pl.BlockSpec((rows, blk, D_MODEL), lambda b, c: (b, order(c), 0)),
        out_shape=jax.ShapeDtypeStruct((bsz, tok, D_MODEL), BF16),
        scratch_shapes=[pltpu.VMEM((rows * N_HEADS // 2, CHUNK, PAIR), F32)],
        compiler_params=_params("parallel", "arbitrary"),
        name="rwkv_bwd" if reverse else "rwkv_fwd",
    )(z, kk, w0, w_up_ext, a0, a_up_ext, k_a)


def _lru_kernel(reverse, n_tiles, x_ref, prev_ref, next_ref, cw_ref, cb_ref, lam_ref,
                wa_ref, ba_ref, wx_ref, bx_ref, h_ref, carry_ref, a_scr, u_scr):
    t = pl.program_id(1)

    @pl.when(t == 0)
    def _():
        carry_ref[...] = jnp.zeros_like(carry_ref)

    ti = _scan_order(reverse, 1, n_tiles, t)
    prev = jnp.where(ti >= 2, prev_ref[0], 0.0)
    nxt = jnp.where((ti >= 1) & (ti <= n_tiles - 2), next_ref[0], 0.0)
    x = x_ref[0]
    ext = jnp.concatenate([prev, x, nxt], axis=0)
    n_ext = TOK_TILE + 2 * SUBLANES
    body = slice(SUBLANES, SUBLANES + TOK_TILE)
    cw = cw_ref[...]
    xc = (cw[0:1] * pltpu.roll(ext, 1, 0)[body] + cw[1:2] * x
          + cw[2:3] * pltpu.roll(ext, n_ext - 1, 0)[body]
          + cw[3:4] * pltpu.roll(ext, n_ext - 2, 0)[body] + cb_ref[...])
    xb = xc.astype(BF16)
    gr = jnp.concatenate([jnp.dot(xb[:, n * LRU_BLOCK:(n + 1) * LRU_BLOCK], wa_ref[0, n],
                                  preferred_element_type=F32) for n in range(LRU_BLOCKS)], axis=1)
    gi = jnp.concatenate([jnp.dot(xb[:, n * LRU_BLOCK:(n + 1) * LRU_BLOCK], wx_ref[0, n],
                                  preferred_element_type=F32) for n in range(LRU_BLOCKS)], axis=1)
    gate_r = _sigmoid(gr + ba_ref[0])
    gate_i = _sigmoid(gi + bx_ref[0])
    lam = lam_ref[0]
    log_sig = jnp.minimum(lam, 0.0) - jnp.log1p(jnp.exp(-jnp.abs(lam)))
    log_a = LRU_C * gate_r * log_sig
    a = jnp.exp(log_a)
    a_scr[...] = a
    u_scr[...] = jnp.sqrt(-jnp.tanh(log_a) * (a * a + 1.0)) * (gate_i * xc)

    row = lax.broadcasted_iota(jnp.int32, (SUBLANES, D_MODEL), 0)
    n_groups = TOK_TILE // SUBLANES

    def group(i, h_in):
        gidx = (n_groups - 1 - i) if reverse else i
        rows = pl.ds(pl.multiple_of(gidx * SUBLANES, SUBLANES), SUBLANES)
        a = a_scr[rows, :]
        u = u_scr[rows, :]
        for s in (1, 2, 4):
            if reverse:
                ok = row < SUBLANES - s
                sh = SUBLANES - s
            else:
                ok = row >= s
                sh = s
            u = jnp.where(ok, a * pltpu.roll(u, sh, 0) + u, u)
            a = jnp.where(ok, a * pltpu.roll(a, sh, 0), a)
        h = u + a * h_in
        h_ref[0, rows, :] = h
        last = h[0:1] if reverse else h[SUBLANES - 1:SUBLANES]
        return jnp.broadcast_to(last, (SUBLANES, D_MODEL))

    carry_ref[...] = lax.fori_loop(0, n_groups, group, carry_ref[...])


def _lru_scan(reverse, d, p_lx, conv_w, conv_b, lam, wa, ba, wx, bx):
    bsz, tok, _ = p_lx.shape
    nt = tok // TOK_TILE
    gpt = TOK_TILE // SUBLANES
    n_groups = tok // SUBLANES
    order = functools.partial(_scan_order, reverse, 1, nt)
    dsel = lambda *_: (d, 0, 0)
    return pl.pallas_call(
        functools.partial(_lru_kernel, reverse, nt),
        grid=(bsz, nt),
        in_specs=[pl.BlockSpec((1, TOK_TILE, D_MODEL), lambda b, t: (b, order(t), 0)),
                  pl.BlockSpec((1, SUBLANES, D_MODEL),
                               lambda b, t: (b, jnp.maximum(order(t) * gpt - 1, 0), 0)),
                  pl.BlockSpec((1, SUBLANES, D_MODEL),
                               lambda b, t: (b, jnp.minimum((order(t) + 1) * gpt, n_groups - 1), 0)),
                  pl.BlockSpec((4, D_MODEL), lambda b, t: (0, 0)),
                  pl.BlockSpec((1, D_MODEL), lambda b, t: (0, 0)),
                  pl.BlockSpec((1, 1, D_MODEL), dsel),
                  pl.BlockSpec((1, LRU_BLOCKS, LRU_BLOCK, LRU_BLOCK), lambda b, t: (d, 0, 0, 0)),
                  pl.BlockSpec((1, 1, D_MODEL), dsel),
                  pl.BlockSpec((1, LRU_BLOCKS, LRU_BLOCK, LRU_BLOCK), lambda b, t: (d, 0, 0, 0)),
                  pl.BlockSpec((1, 1, D_MODEL), dsel)],
        out_specs=pl.BlockSpec((1, TOK_TILE, D_MODEL), lambda b, t: (b, order(t), 0)),
        out_shape=jax.ShapeDtypeStruct((bsz, tok, D_MODEL), F32),
        scratch_shapes=[pltpu.VMEM((SUBLANES, D_MODEL), F32),
                        pltpu.VMEM((TOK_TILE, D_MODEL), F32),
                        pltpu.VMEM((TOK_TILE, D_MODEL), F32)],
        compiler_params=_params("parallel", "arbitrary"),
        name="lru_bwd" if reverse else "lru_fwd",
    )(p_lx, p_lx, p_lx, conv_w, conv_b, lam, wa, ba, wx, bx)


def _mixout_kernel(yf_ref, yb_ref, z_ref, hf_ref, hb_ref, plg_ref, pm_ref, h1_ref, mod_ref,
                   rk_ref, lnw_ref, lnb_ref, gup_ref, wprw_ref, wplru_ref, wout_ref,
                   g2_ref, wg_ref, wu_ref, wd_ref, gf_ref, o_ref):
    e, et = _head_sum_mats()
    inv_n = 1.0 / HEAD_DIM
    y = yf_ref[0].astype(F32) + yb_ref[0].astype(F32)
    mu = _head_expand(_head_reduce(y, e) * inv_n, et)
    yc = y - mu
    var = _head_expand(_head_reduce(yc * yc, e) * inv_n, et)
    yn = yc * lax.rsqrt(var + LN_X_EPS) * lnw_ref[...] + lnb_ref[...]
    r = z_ref[0, :, 0:D_MODEL].astype(F32)
    k = z_ref[0, :, D_MODEL:2 * D_MODEL].astype(F32)
    v = z_ref[0, :, 2 * D_MODEL:3 * D_MODEL].astype(F32)
    bonus = _head_expand(_head_reduce(r * k * rk_ref[...], e), et) * v
    gate = _dot(_sigmoid(z_ref[0, :, COL_G:COL_G + 128].astype(F32)), gup_ref[...])
    y_rw = (yn + bonus) * gate
    lg = plg_ref[0].astype(F32)
    gelu = lg * (0.5 * (1.0 + jnp.tanh(math.sqrt(2.0 / math.pi) * (lg + 0.044715 * (lg * lg * lg)))))
    y_lru = (hf_ref[0] + hb_ref[0]) * gelu
    pm = pm_ref[0].astype(F32)
    merged = (_sigmoid(pm[:, :D_MODEL]) * _dot(y_rw, wprw_ref[...])
              + _sigmoid(pm[:, D_MODEL:]) * _dot(y_lru, wplru_ref[...]))
    m = mod_ref[0, 0]
    x2 = h1_ref[0] + m[5:6] * _dot(merged, wout_ref[...])
    o_ref[0] = _rms(_ffn_math(x2, m, 6, g2_ref[...], wg_ref, wu_ref, wd_ref), gf_ref[...])


def _mixout(y_f, y_b, z, h_f, h_b, p_lg, p_m, h1, mod, r_k, ln_w, ln_b, g_up, w_proj_rw, w_proj_lru, w_out,
            g_ffn, wg, wu, wd, g_final):
    bsz, tok, _ = h1.shape
    nt = tok // TOK_TILE - 1
    lat = lambda w: pl.BlockSpec((1, TOK_TILE, w), lambda b, t: (b, t + 1, 0))
    row = lambda: pl.BlockSpec((1, D_MODEL), lambda b, t: (0, 0))
    return pl.pallas_call(
        _mixout_kernel,
        grid=(bsz, nt),
        in_specs=[lat(D_MODEL), lat(D_MODEL), lat(N_RW), lat(D_MODEL), lat(D_MODEL), lat(D_MODEL),
                  lat(2 * D_MODEL), lat(D_MODEL),
                  pl.BlockSpec((1, 1, N_MOD, D_MODEL), lambda b, t: (b, 1, 0, 0)),
                  row(), row(), row(),
                  _resident((128, D_MODEL)), _resident((D_MODEL, D_MODEL)),
                  _resident((D_MODEL, D_MODEL)), _resident((D_MODEL, D_MODEL)),
                  _resident((1, D_MODEL)),
                  _resident((D_MODEL, D_FF)), _resident((D_MODEL, D_FF)), _resident((D_FF, D_MODEL)),
                  _resident((1, D_MODEL))],
        out_specs=pl.BlockSpec((1, TOK_TILE, D_MODEL), lambda b, t: (b, t, 0)),
        out_shape=jax.ShapeDtypeStruct((bsz, nt * TOK_TILE, D_MODEL), F32),
        compiler_params=_params("parallel", "parallel"),
        name="mixout_ffn2",
    )(y_f, y_b, z, h_f, h_b, p_lg, p_m, h1, mod, r_k, ln_w, ln_b, g_up, w_proj_rw, w_proj_lru, w_out,
      g_ffn, wg, wu, wd, g_final)


def _lora_ext(w_up, d):
    zero = jnp.zeros_like(w_up[0])
    parts = [w_up[0], zero] if d == 0 else [zero, w_up[1]]
    return jnp.concatenate(parts, axis=0)[None].astype(BF16)


def kernel(x, c, ctx, c_ctx, w_mod, b_mod, g_ffn1, ffn1_wg, ffn1_wu, ffn1_wd, g_mix, w_in, rw_mu, rw_w0, rw_w_up, rw_a0, rw_a_up, rw_g_up, rw_k_k, rw_k_a, rw_r_k, rw_ln_w, rw_ln_b, w_proj_rw, lru_conv_w, lru_conv_b, lru_lam, lru_wa, lru_ba, lru_wx, lru_bx, w_proj_lru, w_out, g_ffn2, ffn2_wg, ffn2_wu, ffn2_wd, g_final):
    bsz, seq, d_model = x.shape
    assert d_model == D_MODEL and seq % TOK_TILE == 0 and ctx.shape[1] == TOK_TILE
    assert w_mod.shape[0] == 1 and bsz < SUBLANES
    bf = lambda w: w.astype(BF16)
    row = lambda p: p.reshape(1, -1)

    cc = jnp.zeros((SUBLANES, D_MODEL), F32).at[:bsz].set(c).at[bsz].set(c_ctx)
    mod = _modulation(cc, w_mod[0], row(b_mod[0])).reshape(SUBLANES, N_MOD, D_MODEL)
    mod = jnp.stack([jnp.broadcast_to(mod[bsz], (bsz, N_MOD, D_MODEL)), mod[:bsz]], axis=1)

    h1 = _ffn1(x, ctx, mod, row(g_ffn1[0]), bf(ffn1_wg[0]), bf(ffn1_wu[0]), bf(ffn1_wd[0]))
    z, kk, p_lx, p_lg, p_m = _inproj(h1, mod, row(g_mix[0]), bf(w_in[0]), row(rw_mu[0]), row(rw_k_k[0]))

    n_ctx_chunks = TOK_TILE // CHUNK
    ys, hs = [], []
    for d, reverse in ((0, False), (1, True)):
        ys.append(_rwkv_scan(reverse, z, kk, rw_w0[0, d].reshape(1, 1, -1), _lora_ext(rw_w_up[0], d),
                             rw_a0[0, d].reshape(1, 1, -1), _lora_ext(rw_a_up[0], d), row(rw_k_a[0]),
                             n_ctx_chunks))
        hs.append(_lru_scan(reverse, d, p_lx, lru_conv_w[0], row(lru_conv_b[0]),
                            lru_lam[0][:, None, :], bf(lru_wa[0]), lru_ba[0][:, None, :],
                            bf(lru_wx[0]), lru_bx[0][:, None, :]))

    return _mixout(ys[0], ys[1], z, hs[0], hs[1], p_lg, p_m, h1, mod, row(rw_r_k[0]), row(rw_ln_w[0]),
                   row(rw_ln_b[0]), bf(rw_g_up[0]), bf(w_proj_rw[0]), bf(w_proj_lru[0]), bf(w_out[0]),
                   row(g_ffn2[0]), bf(ffn2_wg[0]), bf(ffn2_wu[0]), bf(ffn2_wd[0]), row(g_final))
```

```python
import functools
import math

import jax
import jax.numpy as jnp
from jax import lax
from jax.experimental import pallas as pl
from jax.experimental.pallas import tpu as pltpu

F32 = jnp.float32
BF16 = jnp.bfloat16

D_MODEL = 1024
D_FF = 2816
N_MOD = 9
NORM_EPS = 1e-6
HEAD_DIM = 64
N_HEADS = D_MODEL // HEAD_DIM
LN_X_EPS = 64e-5
DECAY_SCALE = math.exp(-0.5)
LRU_BLOCKS = 4
LRU_BLOCK = D_MODEL // LRU_BLOCKS
LRU_C = 8.0
GRID_W = 64
N_RW = 3 * D_MODEL + 4 * 64 + 128
COL_LORA = 3 * D_MODEL
COL_G = COL_LORA + 256
N_IN = N_RW + 4 * D_MODEL

TOK_TILE = 256
CHUNK = 64
PAIR = 2 * HEAD_DIM
RWKV_ROWS = 2
RWKV_SUB = 4
PREP_COLS = 1024
MXU_COLS = 256
SHIFT_COL_STARTS = (0, 5 * MXU_COLS, 10 * MXU_COLS, N_RW)
SUBLANES = 8
VMEM_LIMIT = 56 * 1024 * 1024


def _params(*sem):
    return pltpu.CompilerParams(dimension_semantics=sem, vmem_limit_bytes=VMEM_LIMIT)


def _resident(shape):
    nd = len(shape)
    return pl.BlockSpec(shape, lambda *_: (0,) * nd, pipeline_mode=pl.Buffered(1))


def _dot(a, b):
    return jnp.dot(a.astype(BF16), b.astype(BF16), preferred_element_type=F32)


def _dot_nt(a, b):
    return lax.dot_general(a.astype(BF16), b.astype(BF16), (((1,), (1,)), ((), ())),
                           preferred_element_type=F32)


def _split2(x):
    hi = x.astype(BF16)
    lo = (x - hi.astype(F32)).astype(BF16)
    return hi, lo


def _split3(x):
    hi = x.astype(BF16)
    r1 = x - hi.astype(F32)
    mid = r1.astype(BF16)
    lo = (r1 - mid.astype(F32)).astype(BF16)
    return hi, mid, lo


def _rms(x, g):
    return x * lax.rsqrt(jnp.mean(x * x, axis=-1, keepdims=True) + NORM_EPS) * g


def _sigmoid(x):
    return 0.5 * jnp.tanh(0.5 * x) + 0.5


def _head_sum_mats():
    c = lax.broadcasted_iota(jnp.int32, (D_MODEL, 128), 0) // HEAD_DIM
    j = lax.broadcasted_iota(jnp.int32, (D_MODEL, 128), 1)
    e = ((c == j % N_HEADS) & (j < 3 * N_HEADS)).astype(BF16)
    jt = lax.broadcasted_iota(jnp.int32, (128, D_MODEL), 0)
    ct = lax.broadcasted_iota(jnp.int32, (128, D_MODEL), 1) // HEAD_DIM
    et = ((ct == jt % N_HEADS) & (jt < 3 * N_HEADS)).astype(BF16)
    return e, et


def _head_reduce(x, e):
    hi, lo = _split2(x)
    return (jnp.dot(hi, e, preferred_element_type=F32)
            + jnp.dot(lo, e, preferred_element_type=F32))


def _head_expand(s, et):
    hi, mid, lo = _split3(s)
    lane = lax.broadcasted_iota(jnp.int32, s.shape, 1)
    parts = jnp.where(lane < N_HEADS, hi, jnp.where(lane < 2 * N_HEADS, mid, lo))
    return jnp.dot(parts, et, preferred_element_type=F32)


def _mod_kernel(c_ref, w_ref, b_ref, o_ref):
    c = c_ref[...]
    s = c * _sigmoid(c)
    o_ref[...] = _dot(s, w_ref[...]) + b_ref[...]


def _modulation(cc, w_mod, b_mod):
    n = w_mod.shape[1]
    tn = 1152
    return pl.pallas_call(
        _mod_kernel,
        grid=(n // tn,),
        in_specs=[pl.BlockSpec((SUBLANES, D_MODEL), lambda j: (0, 0)),
                  pl.BlockSpec((D_MODEL, tn), lambda j: (0, j)),
                  pl.BlockSpec((1, tn), lambda j: (0, j))],
        out_specs=pl.BlockSpec((SUBLANES, tn), lambda j: (0, j)),
        out_shape=jax.ShapeDtypeStruct((SUBLANES, n), F32),
        compiler_params=_params("parallel"),
        name="modulation",
    )(cc, w_mod, b_mod)


def _ffn_math(h, m, row0, g, wg_ref, wu_ref, wd_ref):
    shift, scale, gate = m[row0:row0 + 1], m[row0 + 1:row0 + 2], m[row0 + 2:row0 + 3]
    hn = (_rms(h, g) * (1.0 + scale) + shift).astype(BF16)
    a = jnp.dot(hn, wg_ref[...], preferred_element_type=F32)
    u = jnp.dot(hn, wu_ref[...], preferred_element_type=F32)
    act = (a * _sigmoid(a) * u).astype(BF16)
    y = jnp.dot(act, wd_ref[...], preferred_element_type=F32)
    return h + 0.5 * gate * y


def _ffn1_kernel(x_ref, ctx_ref, mod_ref, g_ref, wg_ref, wu_ref, wd_ref, o_ref):
    is_ctx = pl.program_id(1) == 0
    h = jnp.where(is_ctx, ctx_ref[0], x_ref[0])
    o_ref[0] = _ffn_math(h, mod_ref[0, 0], 0, g_ref[...], wg_ref, wu_ref, wd_ref)


def _ffn1(x, ctx, mod, g, wg, wu, wd):
    bsz, seq, _ = x.shape
    nt = seq // TOK_TILE + 1
    tile = (1, TOK_TILE, D_MODEL)
    return pl.pallas_call(
        _ffn1_kernel,
        grid=(bsz, nt),
        in_specs=[pl.BlockSpec(tile, lambda b, t: (b, jnp.maximum(t - 1, 0), 0)),
                  pl.BlockSpec(tile, lambda b, t: (b, 0, 0)),
                  pl.BlockSpec((1, 1, N_MOD, D_MODEL), lambda b, t: (b, jnp.minimum(t, 1), 0, 0)),
                  _resident((1, D_MODEL)),
                  _resident((D_MODEL, D_FF)), _resident((D_MODEL, D_FF)), _resident((D_FF, D_MODEL))],
        out_specs=pl.BlockSpec(tile, lambda b, t: (b, t, 0)),
        out_shape=jax.ShapeDtypeStruct((bsz, nt * TOK_TILE, D_MODEL), F32),
        compiler_params=_params("parallel", "parallel"),
        name="ffn1",
    )(x, ctx, mod, g, wg, wu, wd)


def _inproj_kernel(n_tiles, h_ref, dn_ref, mod_ref, g_ref, w_ref, mu_ref, kk_w_ref,
                   z_ref, kk_ref, plx_ref, plg_ref, pm_ref, above_ref, k_ref):
    t = pl.program_id(1)
    is_ctx = t == 0

    @pl.when((pl.program_id(0) == 0) & is_ctx)
    def _():
        above_ref[...] = jnp.zeros_like(above_ref)

    m = mod_ref[0, 0]
    hx = jnp.concatenate([h_ref[0], dn_ref[0]], axis=0)
    xn = (_rms(hx, g_ref[...]) * (1.0 + m[4:5]) + m[3:4]).astype(BF16)
    xc = xn[:TOK_TILE]
    c0, c1, c2 = N_RW, N_RW + D_MODEL, N_RW + 2 * D_MODEL
    plain = [(plx_ref, slice(c0, c1)), (plg_ref, slice(c1, c2)), (pm_ref, slice(c2, N_IN))]

    row = lax.broadcasted_iota(jnp.int32, (TOK_TILE, 1), 0)
    period = jnp.where(is_ctx, TOK_TILE, GRID_W)
    pos = jnp.where(row >= period, row % GRID_W, row)
    first = pos == 0
    last = pos == period - 1
    no_up = ((row < GRID_W) & (t <= 1)) | is_ctx
    no_dn = ((row >= TOK_TILE - GRID_W) & (t == n_tiles - 1)) | is_ctx
    nb_scale = jnp.where(is_ctx, 0.5, 0.25)
    for j in range(len(plain)):
        cols = slice(SHIFT_COL_STARTS[j], SHIFT_COL_STARTS[j + 1])
        pe = jnp.dot(xn, w_ref[:, cols], preferred_element_type=F32)
        p = pe[:TOK_TILE]
        left = jnp.where(first, 0.0, pltpu.roll(p, 1, 0))
        right = jnp.where(last, 0.0, pltpu.roll(p, TOK_TILE - 1, 0))
        up = jnp.where(no_up, 0.0, jnp.concatenate([above_ref[:, cols], p[:TOK_TILE - GRID_W]], axis=0))
        dn = jnp.where(no_dn, 0.0, pe[GRID_W:])
        above_ref[:, cols] = p[TOK_TILE - GRID_W:]
        nb = nb_scale * (up + dn + left + right)
        zf = p + (nb - p) * mu_ref[:, cols]
        z_ref[0, :, cols] = zf.astype(z_ref.dtype)
        lo, hi = max(cols.start, D_MODEL), min(cols.stop, 2 * D_MODEL)
        if lo < hi:
            k_ref[:, lo - D_MODEL:hi - D_MODEL] = zf[:, lo - cols.start:hi - cols.start]
        out_ref, wcols = plain[j]
        out_ref[0] = jnp.dot(xc, w_ref[:, wcols], preferred_element_type=F32).astype(out_ref.dtype)
    e, et = _head_sum_mats()
    kx = k_ref[...] * kk_w_ref[...]
    nrm = jnp.maximum(jnp.sqrt(_head_reduce(kx * kx, e)), 1e-12)
    kk_ref[0] = kx / _head_expand(nrm, et)


def _inproj(h1, mod, g, w_in, mu, k_k):
    bsz, tok, _ = h1.shape
    nt = tok // TOK_TILE
    rows_per_tile = TOK_TILE // GRID_W
    n_rows = tok // GRID_W
    widths = (N_RW, D_MODEL, D_MODEL, D_MODEL, 2 * D_MODEL)
    dtypes = (BF16, F32, F32, BF16, BF16)
    return pl.pallas_call(
        functools.partial(_inproj_kernel, nt),
        grid=(bsz, nt),
        in_specs=[pl.BlockSpec((1, TOK_TILE, D_MODEL), lambda b, t: (b, t, 0)),
                  pl.BlockSpec((1, GRID_W, D_MODEL),
                               lambda b, t: (b, jnp.minimum((t + 1) * rows_per_tile, n_rows - 1), 0)),
                  pl.BlockSpec((1, 1, N_MOD, D_MODEL), lambda b, t: (b, jnp.minimum(t, 1), 0, 0)),
                  _resident((1, D_MODEL)),
                  _resident((D_MODEL, N_IN)),
                  _resident((1, N_RW)),
                  _resident((1, D_MODEL))],
        out_specs=[pl.BlockSpec((1, TOK_TILE, w), lambda b, t: (b, t, 0)) for w in widths],
        out_shape=[jax.ShapeDtypeStruct((bsz, tok, w), dt) for w, dt in zip(widths, dtypes)],
        scratch_shapes=[pltpu.VMEM((GRID_W, N_RW), F32), pltpu.VMEM((TOK_TILE, D_MODEL), F32)],
        compiler_params=_params("arbitrary", "arbitrary"),
        name="inproj",
    )(h1, h1, mod, g, w_in, mu, k_k)


def _scan_order(reverse, n_ctx, n_all, c):
    if not reverse:
        return c
    return jnp.where(c < n_ctx, n_ctx - 1 - c, n_all + n_ctx - 1 - c)


def _rwkv_kernel(reverse, z_ref, kk_ref, w0_ref, wup_ref, a0_ref, aup_ref, ka_ref, y_ref, st_ref):
    c = pl.program_id(1)

    @pl.when(c == 0)
    def _():
        st_ref[...] = jnp.zeros_like(st_ref)

    n_rows = z_ref.shape[0]
    n_sub = z_ref.shape[1] // CHUNK
    n_pairs = N_HEADS // 2
    ri = lax.broadcasted_iota(jnp.int32, (CHUNK, CHUNK), 0)
    ci = lax.broadcasted_iota(jnp.int32, (CHUNK, CHUNK), 1)
    tri = ((ci >= ri) if reverse else (ci <= ri)).astype(BF16)

    def prepare(i, rows, cols):
        col = lambda base: z_ref[i, rows, base + cols.start:base + cols.stop].astype(F32)
        lora = z_ref[i, rows, COL_LORA:COL_LORA + 256].astype(F32)
        lw = -DECAY_SCALE * _sigmoid(
            w0_ref[0, :, cols] + _dot(jnp.tanh(lora[:, :128]), wup_ref[0, :, cols]))
        a = _sigmoid(a0_ref[0, :, cols] + _dot(lora[:, 128:], aup_ref[0, :, cols]))
        lw_hi, lw_lo = _split2(lw)
        cum = jnp.dot(jnp.concatenate([tri, tri], axis=1), jnp.concatenate([lw_hi, lw_lo], axis=0),
                      preferred_element_type=F32)
        cum_last = cum[0:1] if reverse else cum[CHUNK - 1:CHUNK]
        r, k, v = col(0), col(D_MODEL), col(2 * D_MODEL)
        kk = kk_ref[i, rows, cols]
        e_neg = jnp.exp(-cum)
        full = dict(at=kk * jnp.exp(cum - lw), bt=(kk * a) * e_neg,
                    kt=(k * (1.0 + (a - 1.0) * ka_ref[:, cols])) * e_neg,
                    rt=r * jnp.exp(cum), vv=v, gg=jnp.exp(cum_last))
        width = cols.stop - cols.start
        return [{name: x[:, s:s + PAIR] for name, x in full.items()} for s in range(0, width, PAIR)]

    tt = lax.broadcasted_iota(jnp.int32, (CHUNK, PAIR), 0)
    lane = lax.broadcasted_iota(jnp.int32, (CHUNK, PAIR), 1)
    ss = lane % HEAD_DIM
    before = (ss > tt) if reverse else (ss < tt)
    before_eq = (ss >= tt) if reverse else (ss <= tt)
    eye = jnp.where(ss == tt, 1.0, 0.0)
    head0 = lane < HEAD_DIM

    def bd(x):
        xb = x.astype(BF16)
        zero = jnp.zeros_like(xb)
        return jnp.concatenate([jnp.where(head0, xb, zero), jnp.where(head0, zero, xb)], axis=0)

    def tr(x):
        t = bd(x).T
        return jnp.where(head0, t[:CHUNK], t[CHUNK:])

    def solve(q, out):
        n = range(len(q))
        o = [_dot_nt(jnp.concatenate([x["at"], x["rt"]], axis=0),
                     jnp.concatenate([bd(x["bt"]), bd(x["kt"])], axis=0)) for x in q]
        a_ab = [jnp.where(before, x[:CHUNK, :PAIR], 0.0) for x in o]
        a_ak = [jnp.where(before, x[:CHUNK, PAIR:], 0.0) for x in o]
        rbk = [jnp.concatenate([jnp.where(before_eq, x[CHUNK:, :PAIR], 0.0),
                                jnp.where(before_eq, x[CHUNK:, PAIR:], 0.0)], axis=1).astype(BF16) for x in o]
        yield
        inv = [eye - x for x in a_ab]
        pw = [_dot(x, bd(x)) for x in a_ab]
        v_bd = [bd(x["vv"]) for x in q]
        akv = [_dot(a_ak[p], v_bd[p]) for p in n]
        yield
        for _ in range(4):
            zz = [_dot(pw[p], jnp.concatenate([bd(pw[p]), bd(inv[p])], axis=1)) for p in n]
            pw = [x[:, :PAIR] for x in zz]
            inv = [inv[p] + zz[p][:, PAIR:] for p in n]
            yield
        inv = [inv[p] + _dot(pw[p], bd(inv[p])) for p in n]
        yield
        wu = [-_dot(inv[p], jnp.concatenate([bd(q[p]["at"]), bd(akv[p])], axis=1)) for p in n]
        yield
        for p in n:
            g = q[p]["gg"]
            lhs = jnp.concatenate([
                rbk[p],
                jnp.concatenate([tr(q[p]["bt"] * g), tr(q[p]["kt"] * g)], axis=1)], axis=0)
            rhs = jnp.concatenate([
                jnp.concatenate([bd(wu[p][:, :PAIR]), bd(wu[p][:, PAIR:])], axis=1),
                jnp.concatenate([jnp.zeros((PAIR, PAIR), BF16), v_bd[p]], axis=1)], axis=0)
            o = _dot(lhs, rhs)
            out.append((jnp.concatenate([o[:CHUNK, :PAIR] + q[p]["rt"], o[CHUNK:, :PAIR] + eye * g], axis=0),
                        o[:CHUNK, PAIR:], o[CHUNK:, PAIR:]))

    subs = list(range(n_sub))[::-1] if reverse else list(range(n_sub))
    rows_of = {j: slice(j * CHUNK, (j + 1) * CHUNK) for j in subs}
    pieces = [(i, slice(c0, c0 + PREP_COLS)) for i in range(n_rows) for c0 in range(0, D_MODEL, PREP_COLS)]

    def carry(j, loc):
        o2 = [_dot(lhs2, bd(st_ref[p])) for p, (lhs2, _, _) in enumerate(loc)]
        for p, (_, y_loc, n_c) in enumerate(loc):
            st_ref[p] = o2[p][CHUNK:] + n_c
            i, s = p // n_pairs, slice((p % n_pairs) * PAIR, (p % n_pairs + 1) * PAIR)
            y_ref[i, rows_of[j], s] = (o2[p][:CHUNK] + y_loc).astype(y_ref.dtype)

    chains = [prepare(i, rows_of[subs[0]], cols) for i, cols in pieces]
    pending = None
    for idx, j in enumerate(subs):
        q = [x for part in chains for x in part]
        loc, chains = [], []
        todo = list(pieces) if idx + 1 < n_sub else []
        for _ in solve(q, loc):
            if pending is not None:
                carry(*pending)
                pending = None
            if todo:
                i, cols = todo.pop(0)
                chains.append(prepare(i, rows_of[subs[idx + 1]], cols))
        chains += [prepare(i, rows_of[subs[idx + 1]], cols) for i, cols in todo]
        pending = (j, loc)
    carry(*pending)


def _rwkv_scan(reverse, z, kk, w0, w_up_ext, a0, a_up_ext, k_a, n_ctx_chunks):
    bsz, tok, _ = z.shape
    rows = RWKV_ROWS if bsz % RWKV_ROWS == 0 else 1
    assert n_ctx_chunks % RWKV_SUB == 0 and (tok // CHUNK) % RWKV_SUB == 0
    blk = RWKV_SUB * CHUNK
    n_all = tok // blk
    order = functools.partial(_scan_order, reverse, n_ctx_chunks // RWKV_SUB, n_all)
    return pl.pallas_call(
        functools.partial(_rwkv_kernel, reverse),
        grid=(bsz // rows, n_all),
        in_specs=[pl.BlockSpec((rows, blk, N_RW), lambda b, c: (b, order(c), 0)),
                  pl.BlockSpec((rows, blk, D_MODEL), lambda b, c: (b, order(c), 0)),
                  pl.BlockSpec((1, 1, D_MODEL), lambda b, c: (0, 0, 0)),
                  pl.BlockSpec((1, 128, D_MODEL), lambda b, c: (0, 0, 0)),
                  pl.BlockSpec((1, 1, D_MODEL), lambda b, c: (0, 0, 0)),
                  pl.BlockSpec((1, 128, D_MODEL), lambda b, c: (0, 0, 0)),
                  pl.BlockSpec((1, D_MODEL), lambda b, c: (0, 0))],
        out_specs=pl.BlockSpec((rows, blk, D_MODEL), lambda b, c: (b, order(c), 0)),
        out_shape=jax.ShapeDtypeStruct((bsz, tok, D_MODEL), BF16),
        scratch_shapes=[pltpu.VMEM((rows * N_HEADS // 2, CHUNK, PAIR), F32)],
        compiler_params=_params("parallel", "arbitrary"),
        name="rwkv_bwd" if reverse else "rwkv_fwd",
    )(z, kk, w0, w_up_ext, a0, a_up_ext, k_a)


def _lru_kernel(reverse, n_tiles, x_ref, prev_ref, next_ref, cw_ref, cb_ref, lam_ref,
                wa_ref, ba_ref, wx_ref, bx_ref, h_ref, carry_ref, a_scr, u_scr):
    t = pl.program_id(1)

    @pl.when(t == 0)
    def _():
        carry_ref[...] = jnp.zeros_like(carry_ref)

    ti = _scan_order(reverse, 1, n_tiles, t)
    prev = jnp.where(ti >= 2, prev_ref[0], 0.0)
    nxt = jnp.where((ti >= 1) & (ti <= n_tiles - 2), next_ref[0], 0.0)
    x = x_ref[0]
    ext = jnp.concatenate([prev, x, nxt], axis=0)
    n_ext = TOK_TILE + 2 * SUBLANES
    body = slice(SUBLANES, SUBLANES + TOK_TILE)
    cw = cw_ref[...]
    xc = (cw[0:1] * pltpu.roll(ext, 1, 0)[body] + cw[1:2] * x
          + cw[2:3] * pltpu.roll(ext, n_ext - 1, 0)[body]
          + cw[3:4] * pltpu.roll(ext, n_ext - 2, 0)[body] + cb_ref[...])
    xb = xc.astype(BF16)
    gr = jnp.concatenate([jnp.dot(xb[:, n * LRU_BLOCK:(n + 1) * LRU_BLOCK], wa_ref[0, n],
                                  preferred_element_type=F32) for n in range(LRU_BLOCKS)], axis=1)
    gi = jnp.concatenate([jnp.dot(xb[:, n * LRU_BLOCK:(n + 1) * LRU_BLOCK], wx_ref[0, n],
                                  preferred_element_type=F32) for n in range(LRU_BLOCKS)], axis=1)
    gate_r = _sigmoid(gr + ba_ref[0])
    gate_i = _sigmoid(gi + bx_ref[0])
    lam = lam_ref[0]
    log_sig = jnp.minimum(lam, 0.0) - jnp.log1p(jnp.exp(-jnp.abs(lam)))
    log_a = LRU_C * gate_r * log_sig
    a = jnp.exp(log_a)
    a_scr[...] = a
    u_scr[...] = jnp.sqrt(-jnp.tanh(log_a) * (a * a + 1.0)) * (gate_i * xc)

    row = lax.broadcasted_iota(jnp.int32, (SUBLANES, D_MODEL), 0)
    n_groups = TOK_TILE // SUBLANES

    def group(i, h_in):
        gidx = (n_groups - 1 - i) if reverse else i
        rows = pl.ds(pl.multiple_of(gidx * SUBLANES, SUBLANES), SUBLANES)
        a = a_scr[rows, :]
        u = u_scr[rows, :]
        for s in (1, 2, 4):
            if reverse:
                ok = row < SUBLANES - s
                sh = SUBLANES - s
            else:
                ok = row >= s
                sh = s
            u = jnp.where(ok, a * pltpu.roll(u, sh, 0) + u, u)
            a = jnp.where(ok, a * pltpu.roll(a, sh, 0), a)
        h = u + a * h_in
        h_ref[0, rows, :] = h
        last = h[0:1] if reverse else h[SUBLANES - 1:SUBLANES]
        return jnp.broadcast_to(last, (SUBLANES, D_MODEL))

    carry_ref[...] = lax.fori_loop(0, n_groups, group, carry_ref[...])


def _lru_both_kernel(n_tiles, xf_ref, pf_ref, nf_ref, xb_ref, pb_ref, nb_ref, cw_ref, cb_ref, lam_ref,
                     wa_ref, ba_ref, wx_ref, bx_ref, hf_ref, hb_ref, cf_ref, cr_ref, af_scr, uf_scr, ar_scr, ur_scr):
    one = lambda ref, d: ref.at[d:d + 1]
    _lru_kernel(False, n_tiles, xf_ref, pf_ref, nf_ref, cw_ref, cb_ref, one(lam_ref, 0), one(wa_ref, 0),
                one(ba_ref, 0), one(wx_ref, 0), one(bx_ref, 0), hf_ref, cf_ref, af_scr, uf_scr)
    _lru_kernel(True, n_tiles, xb_ref, pb_ref, nb_ref, cw_ref, cb_ref, one(lam_ref, 1), one(wa_ref, 1),
                one(ba_ref, 1), one(wx_ref, 1), one(bx_ref, 1), hb_ref, cr_ref, ar_scr, ur_scr)


def _lru_scans(p_lx, conv_w, conv_b, lam, wa, ba, wx, bx):
    bsz, tok, _ = p_lx.shape
    nt = tok // TOK_TILE
    gpt = TOK_TILE // SUBLANES
    n_groups = tok // SUBLANES

    def tile_specs(reverse):
        order = functools.partial(_scan_order, reverse, 1, nt)
        return [pl.BlockSpec((1, TOK_TILE, D_MODEL), lambda b, t: (b, order(t), 0)),
                pl.BlockSpec((1, SUBLANES, D_MODEL),
                             lambda b, t: (b, jnp.maximum(order(t) * gpt - 1, 0), 0)),
                pl.BlockSpec((1, SUBLANES, D_MODEL),
                             lambda b, t: (b, jnp.minimum((order(t) + 1) * gpt, n_groups - 1), 0))]

    whole = lambda shape: pl.BlockSpec(shape, lambda b, t: (0,) * len(shape))
    gates = (2, LRU_BLOCKS, LRU_BLOCK, LRU_BLOCK)
    tile_scratch = pltpu.VMEM((TOK_TILE, D_MODEL), F32)
    carry_scratch = pltpu.VMEM((SUBLANES, D_MODEL), F32)
    return pl.pallas_call(
        functools.partial(_lru_both_kernel, nt),
        grid=(bsz, nt),
        in_specs=tile_specs(False) + tile_specs(True) + [
            whole((4, D_MODEL)), whole((1, D_MODEL)), whole((2, 1, D_MODEL)),
            whole(gates), whole((2, 1, D_MODEL)), whole(gates), whole((2, 1, D_MODEL))],
        out_specs=[tile_specs(False)[0], tile_specs(True)[0]],
        out_shape=[jax.ShapeDtypeStruct((bsz, tok, D_MODEL), F32)] * 2,
        scratch_shapes=[carry_scratch, carry_scratch, tile_scratch, tile_scratch, tile_scratch, tile_scratch],
        compiler_params=_params("parallel", "arbitrary"),
        name="lru_scans",
    )(p_lx, p_lx, p_lx, p_lx, p_lx, p_lx, conv_w, conv_b, lam, wa, ba, wx, bx)


def _mixout_kernel(yf_ref, yb_ref, z_ref, hf_ref, hb_ref, plg_ref, pm_ref, h1_ref, mod_ref,
                   rk_ref, lnw_ref, lnb_ref, gup_ref, wprw_ref, wplru_ref, wout_ref,
                   g2_ref, wg_ref, wu_ref, wd_ref, gf_ref, o_ref):
    e, et = _head_sum_mats()
    inv_n = 1.0 / HEAD_DIM
    y = yf_ref[0].astype(F32) + yb_ref[0].astype(F32)
    mu = _head_expand(_head_reduce(y, e) * inv_n, et)
    yc = y - mu
    var = _head_expand(_head_reduce(yc * yc, e) * inv_n, et)
    yn = yc * lax.rsqrt(var + LN_X_EPS) * lnw_ref[...] + lnb_ref[...]
    r = z_ref[0, :, 0:D_MODEL].astype(F32)
    k = z_ref[0, :, D_MODEL:2 * D_MODEL].astype(F32)
    v = z_ref[0, :, 2 * D_MODEL:3 * D_MODEL].astype(F32)
    bonus = _head_expand(_head_reduce(r * k * rk_ref[...], e), et) * v
    gate = _dot(_sigmoid(z_ref[0, :, COL_G:COL_G + 128].astype(F32)), gup_ref[...])
    y_rw = (yn + bonus) * gate
    lg = plg_ref[0].astype(F32)
    gelu = lg * (0.5 * (1.0 + jnp.tanh(math.sqrt(2.0 / math.pi) * (lg + 0.044715 * (lg * lg * lg)))))
    y_lru = (hf_ref[0] + hb_ref[0]) * gelu
    pm = pm_ref[0].astype(F32)
    merged = (_sigmoid(pm[:, :D_MODEL]) * _dot(y_rw, wprw_ref[...])
              + _sigmoid(pm[:, D_MODEL:]) * _dot(y_lru, wplru_ref[...]))
    m = mod_ref[0, 0]
    x2 = h1_ref[0] + m[5:6] * _dot(merged, wout_ref[...])
    o_ref[0] = _rms(_ffn_math(x2, m, 6, g2_ref[...], wg_ref, wu_ref, wd_ref), gf_ref[...])


def _mixout(y_f, y_b, z, h_f, h_b, p_lg, p_m, h1, mod, r_k, ln_w, ln_b, g_up, w_proj_rw, w_proj_lru, w_out,
            g_ffn, wg, wu, wd, g_final):
    bsz, tok, _ = h1.shape
    nt = tok // TOK_TILE - 1
    lat = lambda w: pl.BlockSpec((1, TOK_TILE, w), lambda b, t: (b, t + 1, 0))
    row = lambda: pl.BlockSpec((1, D_MODEL), lambda b, t: (0, 0))
    return pl.pallas_call(
        _mixout_kernel,
        grid=(bsz, nt),
        in_specs=[lat(D_MODEL), lat(D_MODEL), lat(N_RW), lat(D_MODEL), lat(D_MODEL), lat(D_MODEL),
                  lat(2 * D_MODEL), lat(D_MODEL),
                  pl.BlockSpec((1, 1, N_MOD, D_MODEL), lambda b, t: (b, 1, 0, 0)),
                  row(), row(), row(),
                  _resident((128, D_MODEL)), _resident((D_MODEL, D_MODEL)),
                  _resident((D_MODEL, D_MODEL)), _resident((D_MODEL, D_MODEL)),
                  _resident((1, D_MODEL)),
                  _resident((D_MODEL, D_FF)), _resident((D_MODEL, D_FF)), _resident((D_FF, D_MODEL)),
                  _resident((1, D_MODEL))],
        out_specs=pl.BlockSpec((1, TOK_TILE, D_MODEL), lambda b, t: (b, t, 0)),
        out_shape=jax.ShapeDtypeStruct((bsz, nt * TOK_TILE, D_MODEL), F32),
        compiler_params=_params("parallel", "parallel"),
        name="mixout_ffn2",
    )(y_f, y_b, z, h_f, h_b, p_lg, p_m, h1, mod, r_k, ln_w, ln_b, g_up, w_proj_rw, w_proj_lru, w_out,
      g_ffn, wg, wu, wd, g_final)


def _lora_ext(w_up, d):
    zero = jnp.zeros_like(w_up[0])
    parts = [w_up[0], zero] if d == 0 else [zero, w_up[1]]
    return jnp.concatenate(parts, axis=0)[None].astype(BF16)


def kernel(x, c, ctx, c_ctx, w_mod, b_mod, g_ffn1, ffn1_wg, ffn1_wu, ffn1_wd, g_mix, w_in, rw_mu, rw_w0, rw_w_up, rw_a0, rw_a_up, rw_g_up, rw_k_k, rw_k_a, rw_r_k, rw_ln_w, rw_ln_b, w_proj_rw, lru_conv_w, lru_conv_b, lru_lam, lru_wa, lru_ba, lru_wx, lru_bx, w_proj_lru, w_out, g_ffn2, ffn2_wg, ffn2_wu, ffn2_wd, g_final):
    bsz, seq, d_model = x.shape
    assert d_model == D_MODEL and seq % TOK_TILE == 0 and ctx.shape[1] == TOK_TILE
    assert w_mod.shape[0] == 1 and bsz < SUBLANES
    bf = lambda w: w.astype(BF16)
    row = lambda p: p.reshape(1, -1)

    cc = jnp.zeros((SUBLANES, D_MODEL), F32).at[:bsz].set(c).at[bsz].set(c_ctx)
    mod = _modulation(cc, w_mod[0], row(b_mod[0])).reshape(SUBLANES, N_MOD, D_MODEL)
    mod = jnp.stack([jnp.broadcast_to(mod[bsz], (bsz, N_MOD, D_MODEL)), mod[:bsz]], axis=1)

    h1 = _ffn1(x, ctx, mod, row(g_ffn1[0]), bf(ffn1_wg[0]), bf(ffn1_wu[0]), bf(ffn1_wd[0]))
    z, kk, p_lx, p_lg, p_m = _inproj(h1, mod, row(g_mix[0]), bf(w_in[0]), row(rw_mu[0]), row(rw_k_k[0]))

    n_ctx_chunks = TOK_TILE // CHUNK
    ys = []
    for d, reverse in ((0, False), (1, True)):
        ys.append(_rwkv_scan(reverse, z, kk, rw_w0[0, d].reshape(1, 1, -1), _lora_ext(rw_w_up[0], d),
                             rw_a0[0, d].reshape(1, 1, -1), _lora_ext(rw_a_up[0], d), row(rw_k_a[0]),
                             n_ctx_chunks))
    hs = _lru_scans(p_lx, lru_conv_w[0], row(lru_conv_b[0]), lru_lam[0][:, None, :], bf(lru_wa[0]),
                    lru_ba[0][:, None, :], bf(lru_wx[0]), lru_bx[0][:, None, :])

    return _mixout(ys[0], ys[1], z, hs[0], hs[1], p_lg, p_m, h1, mod, row(rw_r_k[0]), row(rw_ln_w[0]),
                   row(rw_ln_b[0]), bf(rw_g_up[0]), bf(w_proj_rw[0]), bf(w_proj_lru[0]), bf(w_out[0]),
                   row(g_ffn2[0]), bf(ffn2_wg[0]), bf(ffn2_wu[0]), bf(ffn2_wd[0]), row(g_final))
```
